```python
import math
import jax
import jax.numpy as jnp
from jax import lax
import numpy as np

D_MODEL = 2048
BATCH = 2
SEQ = 4096
DEPTH = 4
DEC_BATCH = 8
DEC_SEQ = 1
PAST_LEN = 16384
PAGE_SIZE = 128

HEAD_DIM = 128
A_HEADS = (3 * D_MODEL) // (8 * HEAD_DIM)
A_KV = 2
A_REP = A_HEADS // A_KV
B_HEADS = (3 * D_MODEL) // (8 * HEAD_DIM)
B_HALF = HEAD_DIM // 2
C_WIDTH = D_MODEL - (A_HEADS + B_HEADS) * HEAD_DIM
C_GROUPS = C_WIDTH // HEAD_DIM
C_GD = C_WIDTH // C_GROUPS
CHUNK = 128
NSA_BLOCK = 64
N_SELECT = 16
WINDOW = 512
N_GATES = 3
Q_BLOCK = 128
N_EXPERTS = 32
TOP_K = 4
D_FF = D_MODEL
SWIGLU_LIMIT = 7.0
SWIGLU_ALPHA = 1.702
MOE_BLOCK = 128
LN_EPS = 1e-5
RMS_EPS = 1e-5
DEEPNORM_ALPHA = (2 * DEPTH) ** 0.25
DEEPNORM_BETA = (8 * DEPTH) ** -0.25
A_KVW = A_KV * HEAD_DIM
IN_SPLITS = (A_HEADS * HEAD_DIM, A_KVW, A_KVW, A_KVW, A_KVW, A_KVW, A_KVW, A_HEADS * N_GATES,
             B_HEADS * HEAD_DIM, B_HEADS * HEAD_DIM, B_HEADS * HEAD_DIM, C_WIDTH, C_WIDTH)
IN_IS_VALUE = (False, False, True, False, True, False, True, False, False, False, True, True, True)
N_IN = sum(IN_SPLITS)

kernel_name = 'hybrid_nsa_diff_sgu_moe_step'


def layer_norm(x, g, b):
    xf = x.astype(jnp.float32)
    mu = jnp.mean(xf, axis=-1, keepdims=True)
    var = jnp.mean(jnp.square(xf - mu), axis=-1, keepdims=True)
    return ((xf - mu) * lax.rsqrt(var + LN_EPS) * g + b).astype(x.dtype)


def rms_norm(x, g):
    xf = x.astype(jnp.float32)
    return (xf * lax.rsqrt(jnp.mean(xf * xf, axis=-1, keepdims=True) + RMS_EPS) * g).astype(x.dtype)


def masked_softmax(s, mask):
    s = jnp.where(mask, s, -jnp.inf)
    m = jnp.max(s, axis=-1, keepdims=True)
    m = jnp.where(jnp.isfinite(m), m, 0.0)
    e = jnp.where(mask, jnp.exp(s - m), 0.0)
    return e / jnp.maximum(jnp.sum(e, axis=-1, keepdims=True), 1e-30)


def alibi_slopes():
    n = A_HEADS + B_HEADS
    sl = jnp.asarray(2.0 ** (-8.0 * np.arange(1, n + 1) / n), dtype=jnp.float32)
    return sl[0::2].reshape(A_KV, A_REP), sl[1::2]


def seq_slice(a, start, size, axis):
    return lax.dynamic_slice_in_dim(a, start, size, axis)


def unblock(o):
    nb, b, qb = o.shape[:3]
    return jnp.moveaxis(o, 0, 1).reshape(b, nb * qb, *o.shape[3:])


def project_heads(x, w_in):
    b, t, _ = x.shape
    offs = np.cumsum(IN_SPLITS)[:-1].tolist()
    aq, ack, acv, ask, asv, awk, awv, ag, bq, bk, bv, cu, cv = jnp.split(x @ w_in, offs, axis=-1)
    kvh = lambda a: a.reshape(b, t, A_KV, HEAD_DIM)
    nsa_q = aq.reshape(b, t, A_KV, A_REP, HEAD_DIM)
    nsa_kv = jnp.stack([kvh(ack), kvh(acv), kvh(ask), kvh(asv)], axis=2)
    win_kv = jnp.stack([kvh(awk), kvh(awv)], axis=2)
    gates = jax.nn.sigmoid(ag.astype(jnp.float32)).reshape(b, t, A_KV, A_REP, N_GATES).astype(x.dtype)
    diff_q = bq.reshape(b, t, B_HEADS, 2, B_HALF)
    diff_kv = jnp.stack([bk.reshape(b, t, B_HEADS, HEAD_DIM), bv.reshape(b, t, B_HEADS, HEAD_DIM)], axis=2)
    return nsa_q, nsa_kv, win_kv, gates, diff_q, diff_kv, jax.nn.gelu(cu), jax.nn.gelu(cv)


def gqa_attend(q, k, v, slopes, q_pos, k_pos, allowed):
    s = jnp.einsum('bqgrd,bkgd->bgrqk', q, k).astype(jnp.float32) * (q.shape[-1] ** -0.5)
    dist = (q_pos[:, None] - k_pos[None, :]).astype(jnp.float32)
    p = masked_softmax(s - slopes[:, :, None, None] * dist, allowed)
    return jnp.einsum('bgrqk,bkgd->bqgrd', p.astype(v.dtype), v), p


def window_mask(q_pos, k_pos):
    d = q_pos[:, None] - k_pos[None, :]
    return (d >= 0) & (d < WINDOW) & (k_pos[None, :] >= 0)


def compress_blocks(rows, w_pos):
    b, tk, g, d = rows.shape
    nb = tk // NSA_BLOCK
    rb = rows[:, :nb * NSA_BLOCK].reshape(b, nb, NSA_BLOCK, g, d)
    return jnp.einsum('bnlgd,ld->bngd', rb, w_pos)


def to_blocks(rows, n_blocks):
    b, tk, g, d = rows.shape
    rows = jnp.pad(rows, ((0, 0), (0, n_blocks * NSA_BLOCK - tk), (0, 0), (0, 0)))
    return jnp.transpose(rows.reshape(b, n_blocks, NSA_BLOCK, g, d), (0, 3, 1, 2, 4))


def nsa_compressed(q, k_rows, v_rows, w_pk, w_pv, slopes, q_pos):
    k_cmp = compress_blocks(k_rows, w_pk)
    v_cmp = compress_blocks(v_rows, w_pv)
    end = (jnp.arange(k_cmp.shape[1]) + 1) * NSA_BLOCK - 1
    return gqa_attend(q, k_cmp, v_cmp, slopes, q_pos, end, end[None, :] <= q_pos[:, None])


def select_blocks(p_cmp, q_pos, n_blocks):
    imp = jnp.sum(p_cmp, axis=2)
    imp = jnp.pad(imp, ((0, 0), (0, 0), (0, 0), (0, n_blocks - imp.shape[-1])))
    blk = jnp.arange(n_blocks)[None, :]
    cur = (q_pos // NSA_BLOCK)[:, None]
    score = jnp.where(blk == cur, jnp.inf, jnp.where(blk > cur, -jnp.inf, imp))
    _, idx = lax.top_k(score, min(N_SELECT, n_blocks))
    return idx


def nsa_selected(q, k_blocks, v_blocks, idx, slopes, q_pos):
    b, tq, g, r, d = q.shape
    n = idx.shape[-1]
    bi = jnp.arange(b)[:, None, None, None]
    gi = jnp.arange(g)[None, :, None, None]
    kg = k_blocks[bi, gi, idx]
    vg = v_blocks[bi, gi, idx].reshape(b, g, tq, n * NSA_BLOCK, d)
    s = jnp.einsum('bqgrd,bgqnld->bgrqnl', q, kg).astype(jnp.float32) * (d ** -0.5)
    k_pos = idx[..., None] * NSA_BLOCK + jnp.arange(NSA_BLOCK)
    dist = (q_pos[None, None, :, None, None] - k_pos).astype(jnp.float32)[:, :, None]
    s = (s - slopes[None, :, :, None, None, None] * dist).reshape(b, g, r, tq, n * NSA_BLOCK)
    p = masked_softmax(s, (dist >= 0).reshape(b, g, 1, tq, n * NSA_BLOCK))
    return jnp.einsum('bgrqm,bgqmd->bqgrd', p.astype(vg.dtype), vg)


def nsa_combine(gates, o_cmp, o_slc, o_win):
    return gates[..., 0:1] * o_cmp + gates[..., 1:2] * o_slc + gates[..., 2:3] * o_win


def diff_lambda(lq1, lk1, lq2, lk2, lam_init):
    f = jnp.float32
    return (jnp.exp(jnp.sum(lq1.astype(f) * lk1.astype(f))) - jnp.exp(jnp.sum(lq2.astype(f) * lk2.astype(f)))
            + lam_init)


def diff_attend(q, k, v, lam, slopes, q_pos, k_pos, allowed):
    s = jnp.einsum('bqhcd,bkhcd->bhcqk', q, k).astype(jnp.float32) * (B_HALF ** -0.5)
    dist = (q_pos[:, None] - k_pos[None, :]).astype(jnp.float32)
    p = masked_softmax(s - slopes[None, :, None, None, None] * dist, allowed)
    a = p[:, :, 0] - lam * p[:, :, 1]
    return jnp.einsum('bhqk,bkhd->bqhd', a.astype(v.dtype), v)


def sgu(u, v, ln_g, ln_b, w_s, b_s):
    b, t, _ = u.shape
    v = layer_norm(v, ln_g, ln_b)
    nc = -(-t // CHUNK)
    vc = jnp.pad(v, ((0, 0), (0, nc * CHUNK - t), (0, 0))).reshape(b, nc, CHUNK, C_GROUPS, C_GD)
    w = w_s * jnp.tril(jnp.ones((CHUNK, CHUNK), w_s.dtype))
    s = jnp.einsum('gts,bnsgd->bntgd', w, vc) + b_s.T[None, None, :, :, None]
    return u * s.reshape(b, nc * CHUNK, C_WIDTH)[:, :t], v


def merge_heads(o_nsa, o_diff, o_sgu, diff_g, lam_init, w_out):
    b, t = o_nsa.shape[:2]
    o_diff = rms_norm(o_diff, diff_g) * (1.0 - lam_init)
    h = jnp.concatenate([o_nsa.reshape(b, t, -1), o_diff.reshape(b, t, -1), o_sgu], axis=-1)
    return h @ w_out


def mixers_prompt(x, w_in, cmp_wk, cmp_wv, lam, lam_init, diff_g, sgu_ln_g, sgu_ln_b, sgu_w, sgu_b, w_out,
                  sl_a, sl_b):
    b, t, _ = x.shape
    nsa_q, nsa_kv, win_kv, gates, diff_q, diff_kv, u, v = project_heads(x, w_in)
    pos = jnp.arange(t)
    qb_ids = jnp.arange(t // Q_BLOCK)
    o_cmp, p_cmp = nsa_compressed(nsa_q, nsa_kv[:, :, 0], nsa_kv[:, :, 1], cmp_wk, cmp_wv, sl_a, pos)
    n_blocks = -(-t // NSA_BLOCK)
    idx = select_blocks(p_cmp, pos, n_blocks)
    k_blocks = to_blocks(nsa_kv[:, :, 2], n_blocks)
    v_blocks = to_blocks(nsa_kv[:, :, 3], n_blocks)

    def sel_block(i):
        s0 = i * Q_BLOCK
        return nsa_selected(seq_slice(nsa_q, s0, Q_BLOCK, 1), k_blocks, v_blocks,
                            seq_slice(idx, s0, Q_BLOCK, 2), sl_a, s0 + jnp.arange(Q_BLOCK))
    o_slc = unblock(lax.map(sel_block, qb_ids))
    win_pad = jnp.pad(win_kv, ((0, 0), (WINDOW, 0), (0, 0), (0, 0), (0, 0)))

    def win_block(i):
        s0 = i * Q_BLOCK
        qp = s0 + jnp.arange(Q_BLOCK)
        kp = s0 - WINDOW + jnp.arange(WINDOW + Q_BLOCK)
        kv = seq_slice(win_pad, s0, WINDOW + Q_BLOCK, 1)
        o, _ = gqa_attend(seq_slice(nsa_q, s0, Q_BLOCK, 1), kv[:, :, 0], kv[:, :, 1], sl_a, qp, kp,
                          window_mask(qp, kp))
        return o
    o_win = unblock(lax.map(win_block, qb_ids))
    o_nsa = nsa_combine(gates, o_cmp, o_slc, o_win)
    dk = diff_kv[:, :, 0].reshape(b, t, B_HEADS, 2, B_HALF)
    dv = diff_kv[:, :, 1]

    def diff_block(i):
        s0 = i * Q_BLOCK
        qp = s0 + jnp.arange(Q_BLOCK)
        return diff_attend(seq_slice(diff_q, s0, Q_BLOCK, 1), dk, dv, lam, sl_b, qp, pos,
                           pos[None, :] <= qp[:, None])
    o_diff = unblock(lax.map(diff_block, qb_ids))
    o_sgu, _ = sgu(u, v, sgu_ln_g, sgu_ln_b, sgu_w, sgu_b)
    y = merge_heads(o_nsa, o_diff, o_sgu, diff_g, lam_init, w_out)
    n_win = min(WINDOW, t)
    return y, nsa_kv, diff_kv, win_kv[:, t - n_win:]


def mixers_sample(x, past_nsa, past_diff, win_buf, w_in, cmp_wk, cmp_wv, lam, lam_init, diff_g,
                  sgu_ln_g, sgu_ln_b, sgu_w, sgu_b, w_out, sl_a, sl_b):
    b, t, _ = x.shape
    p_len = past_nsa.shape[1]
    nsa_q, nsa_kv, win_kv, gates, diff_q, diff_kv, u, v = project_heads(x, w_in)
    q_pos = p_len + jnp.arange(t)
    tk = p_len + t
    nsa_all = jnp.concatenate([past_nsa, nsa_kv], axis=1)
    o_cmp, p_cmp = nsa_compressed(nsa_q, nsa_all[:, :, 0], nsa_all[:, :, 1], cmp_wk, cmp_wv, sl_a, q_pos)
    n_blocks = -(-tk // NSA_BLOCK)
    idx = select_blocks(p_cmp, q_pos, n_blocks)
    o_slc = nsa_selected(nsa_q, to_blocks(nsa_all[:, :, 2], n_blocks), to_blocks(nsa_all[:, :, 3], n_blocks),
                         idx, sl_a, q_pos)
    n_win = win_buf.shape[1]
    win_all = jnp.concatenate([win_buf, win_kv], axis=1)
    kp = p_len - n_win + jnp.arange(n_win + t)
    o_win, _ = gqa_attend(nsa_q, win_all[:, :, 0], win_all[:, :, 1], sl_a, q_pos, kp, window_mask(q_pos, kp))
    o_nsa = nsa_combine(gates, o_cmp, o_slc, o_win)
    diff_all = jnp.concatenate([past_diff, diff_kv], axis=1)
    k_pos = jnp.arange(tk)
    o_diff = diff_attend(diff_q, diff_all[:, :, 0].reshape(b, tk, B_HEADS, 2, B_HALF), diff_all[:, :, 1],
                         lam, sl_b, q_pos, k_pos, k_pos[None, :] <= q_pos[:, None])
    o_sgu, v_rows = sgu(u, v, sgu_ln_g, sgu_ln_b, sgu_w, sgu_b)
    y = merge_heads(o_nsa, o_diff, o_sgu, diff_g, lam_init, w_out)
    return y, nsa_kv, diff_kv, win_all[:, win_all.shape[1] - n_win:], v_rows


def clamped_swiglu(h):
    glu = jnp.minimum(h[..., 0::2], SWIGLU_LIMIT)
    lin = jnp.clip(h[..., 1::2], -SWIGLU_LIMIT, SWIGLU_LIMIT)
    return glu * jax.nn.sigmoid(SWIGLU_ALPHA * glu) * (lin + 1.0)


def moe_ffn(x, l, router_w, router_b, w_up, b_up, w_down, b_down):
    b, t, d = x.shape
    m = b * t
    xt = x.reshape(m, d)
    logits = (xt @ router_w + router_b).astype(jnp.float32)
    top_v, top_e = lax.top_k(logits, TOP_K)
    gate = jax.nn.softmax(top_v, axis=-1)
    blk = min(MOE_BLOCK, m)
    n_assign = m * TOP_K
    flat_e = top_e.reshape(-1)
    order = jnp.argsort(flat_e)
    se = flat_e[order]
    stok = (jnp.arange(n_assign) // TOP_K)[order].astype(jnp.int32)
    sg = gate.reshape(-1)[order]
    counts = jnp.bincount(flat_e, length=N_EXPERTS)
    padded = (counts + blk - 1) // blk * blk
    start = jnp.cumsum(counts) - counts
    pend = jnp.cumsum(padded)
    dest = (pend - padded)[se] + jnp.arange(n_assign) - start[se]
    n_blocks = -(-n_assign // blk) + N_EXPERTS
    row_tok = jnp.full((n_blocks * blk,), m, jnp.int32).at[dest].set(stok)
    row_gate = jnp.zeros((n_blocks * blk,), jnp.float32).at[dest].set(sg)
    block_e = jnp.minimum(jnp.sum(jnp.arange(n_blocks)[:, None] * blk >= pend[None, :], axis=1), N_EXPERTS - 1)
    x_rows = jnp.concatenate([xt, jnp.zeros((1, d), xt.dtype)], axis=0)[row_tok].reshape(n_blocks, blk, d)

    def expert_block(args):
        xb, e = args
        h = clamped_swiglu(xb @ w_up[l, e] + b_up[l, e])
        return h @ w_down[l, e] + b_down[l, e]
    y_rows = lax.map(expert_block, (x_rows, block_e)).reshape(n_blocks * blk, d)
    y = jnp.zeros((m + 1, d), y_rows.dtype).at[row_tok].add(y_rows * row_gate[:, None].astype(y_rows.dtype))
    return y[:m].reshape(b, t, d)


def setup_inputs(seed: int = 0) -> dict:
    key = jax.random.key(seed)
    k = jax.random.split(key, 32)
    f32 = jnp.float32

    def nrm(kk, shape, scale):
        return jax.random.normal(kk, shape, f32) * scale

    def gain(kk, shape):
        return 1.0 + nrm(kk, shape, 0.05)

    n_pages = PAST_LEN // PAGE_SIZE
    n_used = DEC_BATCH * n_pages
    n_pool = n_used + max(1, n_used // 4)
    n_win = min(WINDOW, PAST_LEN)
    page_table = jax.random.permutation(k[5], n_pool)[:n_used].reshape(DEC_BATCH, n_pages).astype(jnp.int32)
    col_scale = jnp.asarray(np.concatenate(
        [np.full((w,), DEEPNORM_BETA if is_v else 1.0, np.float32) for w, is_v in zip(IN_SPLITS, IN_IS_VALUE)]))
    return {
        'x_prompt': nrm(k[0], (BATCH, SEQ, D_MODEL), 1.0),
        'x_sample': nrm(k[1], (DEC_BATCH, DEC_SEQ, D_MODEL), 1.0),
        'cache_nsa': nrm(k[2], (DEPTH, n_pool, PAGE_SIZE, 4, A_KV, HEAD_DIM), 1.0),
        'cache_diff': nrm(k[3], (DEPTH, n_pool, PAGE_SIZE, 2, B_HEADS, HEAD_DIM), 1.0),
        'state_win': nrm(k[4], (DEPTH, DEC_BATCH, n_win, 2, A_KV, HEAD_DIM), 1.0),
        'page_table': page_table,
        'ln_in_g': gain(k[6], (D_MODEL,)),
        'ln_in_b': nrm(k[7], (D_MODEL,), 0.02),
        'w_in': nrm(k[8], (DEPTH, D_MODEL, N_IN), D_MODEL ** -0.5) * col_scale,
        'cmp_wk': (1.0 + nrm(k[9], (DEPTH, NSA_BLOCK, HEAD_DIM), 0.5)) / NSA_BLOCK,
        'cmp_wv': (1.0 + nrm(k[10], (DEPTH, NSA_BLOCK, HEAD_DIM), 0.5)) / NSA_BLOCK,
        'lam_q1': nrm(k[11], (DEPTH, B_HALF), 0.1),
        'lam_k1': nrm(k[12], (DEPTH, B_HALF), 0.1),
        'lam_q2': nrm(k[13], (DEPTH, B_HALF), 0.1),
        'lam_k2': nrm(k[14], (DEPTH, B_HALF), 0.1),
        'diff_norm_g': gain(k[15], (DEPTH, HEAD_DIM)),
        'sgu_ln_g': gain(k[16], (DEPTH, C_WIDTH)),
        'sgu_ln_b': nrm(k[17], (DEPTH, C_WIDTH), 0.02),
        'sgu_w': nrm(k[18], (DEPTH, C_GROUPS, CHUNK, CHUNK), 0.5 * CHUNK ** -0.5),
        'sgu_b': 1.0 + nrm(k[19], (DEPTH, C_GROUPS, CHUNK), 0.1),
        'w_out': nrm(k[20], (DEPTH, D_MODEL, D_MODEL), DEEPNORM_BETA * D_MODEL ** -0.5),
        'ln1_g': gain(k[21], (DEPTH, D_MODEL)),
        'ln1_b': nrm(k[22], (DEPTH, D_MODEL), 0.02),
        'router_w': nrm(k[23], (DEPTH, D_MODEL, N_EXPERTS), D_MODEL ** -0.5),
        'router_b': nrm(k[24], (DEPTH, N_EXPERTS), 0.01),
        'w_up': nrm(k[25], (DEPTH, N_EXPERTS, D_MODEL, 2 * D_FF), DEEPNORM_BETA * D_MODEL ** -0.5),
        'b_up': nrm(k[26], (DEPTH, N_EXPERTS, 2 * D_FF), 0.01),
        'w_down': nrm(k[27], (DEPTH, N_EXPERTS, D_FF, D_MODEL), DEEPNORM_BETA * D_FF ** -0.5),
        'b_down': nrm(k[28], (DEPTH, N_EXPERTS, D_MODEL), 0.01),
        'ln2_g': gain(k[29], (DEPTH, D_MODEL)),
        'ln2_b': nrm(k[30], (DEPTH, D_MODEL), 0.02),
    }


def reference(x_prompt, x_sample, cache_nsa, cache_diff, state_win, page_table,
              ln_in_g, ln_in_b, w_in, cmp_wk, cmp_wv, lam_q1, lam_k1, lam_q2, lam_k2, diff_norm_g,
              sgu_ln_g, sgu_ln_b, sgu_w, sgu_b, w_out, ln1_g, ln1_b,
              router_w, router_b, w_up, b_up, w_down, b_down, ln2_g, ln2_b):
    sl_a, sl_b = alibi_slopes()
    dec_b = x_sample.shape[0]
    past_len = page_table.shape[1] * cache_nsa.shape[2]

    def gather_past(pool, l):
        return pool[l, page_table].reshape(dec_b, past_len, *pool.shape[3:])

    xp = layer_norm(x_prompt, ln_in_g, ln_in_b)
    xs = layer_norm(x_sample, ln_in_g, ln_in_b)
    nsa_p, nsa_s, diff_p, diff_s, win_p, win_s, sgu_s = [], [], [], [], [], [], []
    for l in range(DEPTH):
        lam_init = 0.8 - 0.6 * math.exp(-0.3 * l)
        lam = diff_lambda(lam_q1[l], lam_k1[l], lam_q2[l], lam_k2[l], lam_init)
        hp, r_nsa, r_diff, r_win = mixers_prompt(xp, w_in[l], cmp_wk[l], cmp_wv[l], lam, lam_init, diff_norm_g[l],
                                                 sgu_ln_g[l], sgu_ln_b[l], sgu_w[l], sgu_b[l], w_out[l], sl_a, sl_b)
        xp = layer_norm(DEEPNORM_ALPHA * xp + hp, ln1_g[l], ln1_b[l])
        xp = layer_norm(DEEPNORM_ALPHA * xp + moe_ffn(xp, l, router_w[l], router_b[l], w_up, b_up, w_down, b_down),
                        ln2_g[l], ln2_b[l])
        nsa_p.append(r_nsa)
        diff_p.append(r_diff)
        win_p.append(r_win)
        hs, s_nsa, s_diff, s_win, s_v = mixers_sample(xs, gather_past(cache_nsa, l), gather_past(cache_diff, l),
                                                      state_win[l], w_in[l], cmp_wk[l], cmp_wv[l], lam, lam_init,
                                                      diff_norm_g[l], sgu_ln_g[l], sgu_ln_b[l], sgu_w[l], sgu_b[l],
                                                      w_out[l], sl_a, sl_b)
        xs = layer_norm(DEEPNORM_ALPHA * xs + hs, ln1_g[l], ln1_b[l])
        xs = layer_norm(DEEPNORM_ALPHA * xs + moe_ffn(xs, l, router_w[l], router_b[l], w_up, b_up, w_down, b_down),
                        ln2_g[l], ln2_b[l])
        nsa_s.append(s_nsa)
        diff_s.append(s_diff)
        win_s.append(s_win)
        sgu_s.append(s_v)
    return (xp, xs, jnp.stack(nsa_p), jnp.stack(nsa_s), jnp.stack(diff_p), jnp.stack(diff_s),
            jnp.stack(win_p), jnp.stack(win_s), jnp.stack(sgu_s))
```

```python
import functools
import math

import numpy as np
import jax
import jax.numpy as jnp
from jax import lax
from jax.experimental import pallas as pl
from jax.experimental.pallas import tpu as pltpu

F32 = jnp.float32
BF16 = jnp.bfloat16
I32 = jnp.int32

HEAD_DIM = 128
A_KV = 2
A_REP = 3
A_HEADS = A_KV * A_REP
B_HEADS = 6
B_HALF = HEAD_DIM // 2
C_GROUPS = 4
C_WIDTH = C_GROUPS * HEAD_DIM
CHUNK = 128
NSA_BLOCK = 64
N_SELECT = 16
WINDOW = 512
N_GATES = 3
N_EXPERTS = 32
TOP_K = 4
SWIGLU_LIMIT = 7.0
SWIGLU_ALPHA = 1.702
LN_EPS = 1e-5
RMS_EPS = 1e-5
LANE = 128
NEG = -1e30

_SL = (2.0 ** (-8.0 * np.arange(1, A_HEADS + B_HEADS + 1) / (A_HEADS + B_HEADS))).astype(np.float32)
SL_A = _SL[0::2].reshape(A_KV, A_REP)
SL_B = _SL[1::2]

CB_AQ = 0
CB_NSA = 6
CB_WIN = 14
CB_BQ = 18
CB_BK = 24
CB_BV = 30
CB_CU = 36
CB_CV = 40
CB_GATE = 44
N_PROJ = 45 * LANE
N_GATE_COLS = A_HEADS * N_GATES
GATE_ORIG = 2304

NT_DIMS = (((1,), (1,)), ((), ()))


def _cparams(sem, vmem_mb=None):
    kw = dict(dimension_semantics=sem)
    if vmem_mb is not None:
        kw["vmem_limit_bytes"] = vmem_mb * 1024 * 1024
    return pltpu.CompilerParams(**kw)


def _pick_tile(n, candidates):
    for c in candidates:
        if n % c == 0:
            return c
    return n


def _ln_rows(x, g, b):
    mu = jnp.mean(x, axis=-1, keepdims=True)
    xc = x - mu
    var = jnp.mean(xc * xc, axis=-1, keepdims=True)
    return xc * lax.rsqrt(var + LN_EPS) * g + b


def _select_by_index(idx, values):
    out = values[-1]
    for j in range(len(values) - 2, -1, -1):
        out = jnp.where(idx == j, values[j], out)
    return out


def _ln_in_kernel(x_ref, g_ref, b_ref, of_ref, ob_ref):
    y = _ln_rows(x_ref[...], g_ref[...], b_ref[...])
    of_ref[...] = y
    ob_ref[...] = y.astype(BF16)


def _ln_in(x, g, b):
    mt, d = x.shape
    tm = _pick_tile(mt, (256, 128, 8))
    row = pl.BlockSpec((tm, d), lambda i: (i, 0))
    vec = pl.BlockSpec((1, d), lambda i: (0, 0))
    return pl.pallas_call(
        _ln_in_kernel, grid=(mt // tm,), in_specs=[row, vec, vec], out_specs=[row, row],
        out_shape=[jax.ShapeDtypeStruct((mt, d), F32), jax.ShapeDtypeStruct((mt, d), BF16)],
        compiler_params=_cparams(("arbitrary",)), name="ln_in")(x, g.reshape(1, d), b.reshape(1, d))


def _proj_kernel(x_ref, w_ref, o_ref):
    o_ref[...] = jnp.dot(x_ref[...], w_ref[...], preferred_element_type=F32)


def _proj(xb, w_all, layer):
    mt, d = xb.shape
    n = w_all.shape[-1]
    tm = _pick_tile(mt, (768, 512, 256, 128, 8))
    tn = _pick_tile(n, (1152, 640, 128))
    return pl.pallas_call(
        _proj_kernel, grid=(n // tn, mt // tm),
        in_specs=[pl.BlockSpec((tm, d), lambda j, i: (i, 0)),
                  pl.BlockSpec((None, d, tn), lambda j, i: (layer, 0, j))],
        out_specs=pl.BlockSpec((tm, tn), lambda j, i: (i, j)),
        out_shape=jax.ShapeDtypeStruct((mt, n), F32),
        compiler_params=_cparams(("arbitrary", "arbitrary"), 48), name="in_proj")(xb, w_all)


def _outproj_kernel(hn_ref, hd_ref, hs_ref, wn_ref, wd_ref, ws_ref, x_ref, g_ref, b_ref, rw_ref, rb_ref,
                    xf_ref, lg_ref, *, alpha):
    h = jnp.dot(hn_ref[...], wn_ref[...], preferred_element_type=F32)
    h = h + jnp.dot(hd_ref[...], wd_ref[...], preferred_element_type=F32)
    h = h + jnp.dot(hs_ref[...], ws_ref[...], preferred_element_type=F32)
    y = _ln_rows(alpha * x_ref[...] + h, g_ref[...], b_ref[...])
    xf_ref[...] = y
    yh = y.astype(BF16)
    yl = (y - yh.astype(F32)).astype(BF16)
    rw = rw_ref[...]
    wh = rw.astype(BF16)
    wl = (rw - wh.astype(F32)).astype(BF16)
    lg = jnp.dot(yh, wh, preferred_element_type=F32)
    lg = lg + jnp.dot(yh, wl, preferred_element_type=F32)
    lg = lg + jnp.dot(yl, wh, preferred_element_type=F32)
    lg_ref[...] = lg + rb_ref[...]


def _outproj(hn, hd, hs, w_out_b, x, g, b, rw_pad, rb_pad, layer, alpha):
    mt, d = x.shape
    tm = _pick_tile(mt, (256, 128, 8))
    wn, wdw, wsw = hn.shape[1], hd.shape[1], hs.shape[1]
    assert wn == wdw and (wn + wdw) % wsw == 0
    row = lambda w: pl.BlockSpec((tm, w), lambda i: (i, 0))
    vec = pl.BlockSpec((None, 1, d), lambda i: (layer, 0, 0))
    return pl.pallas_call(
        functools.partial(_outproj_kernel, alpha=alpha), grid=(mt // tm,),
        in_specs=[row(wn), row(wdw), row(wsw),
                  pl.BlockSpec((None, wn, d), lambda i: (layer, 0, 0)),
                  pl.BlockSpec((None, wdw, d), lambda i: (layer, 1, 0)),
                  pl.BlockSpec((None, wsw, d), lambda i: (layer, (wn + wdw) // wsw, 0)),
                  row(d), vec, vec,
                  pl.BlockSpec((None, d, LANE), lambda i: (layer, 0, 0)),
                  pl.BlockSpec((None, 1, LANE), lambda i: (layer, 0, 0))],
        out_specs=[row(d), row(LANE)],
        out_shape=[jax.ShapeDtypeStruct((mt, d), F32), jax.ShapeDtypeStruct((mt, LANE), F32)],
        compiler_params=_cparams(("arbitrary",), 48), name="out_proj_ln")(
            hn, hd, hs, w_out_b, w_out_b, w_out_b, x, g, b, rw_pad, rb_pad)


def _route_kernel(lg_ref, e_ref, r_ref, g_ref, cnt_ref, carry, *, tb, n_valid):
    i = pl.program_id(0)

    @pl.when(i == 0)
    def _():
        carry[...] = jnp.zeros_like(carry)

    work = lg_ref[...]
    lane = lax.broadcasted_iota(I32, (tb, LANE), 1)
    lane_f = lane.astype(F32)
    row = i * tb + lax.broadcasted_iota(I32, (tb, 1), 0)
    valid = row < n_valid
    vals, idxs = [], []
    for _ in range(TOP_K):
        m = jnp.max(work, axis=-1, keepdims=True)
        idx = jnp.min(jnp.where(work == m, lane_f, float(LANE)), axis=-1, keepdims=True)
        vals.append(m)
        idxs.append(idx)
        work = jnp.where(lane_f == idx, -jnp.inf, work)
    ex = [jnp.exp(v - vals[0]) for v in vals]
    den = ex[0] + ex[1] + ex[2] + ex[3]
    hot = [(lane_f == idx) & valid for idx in idxs]
    a = jnp.zeros((tb, LANE), F32)
    for hk in hot:
        a = a + jnp.where(hk, 1.0, 0.0)
    tri = (lax.broadcasted_iota(I32, (tb, tb), 0) > lax.broadcasted_iota(I32, (tb, tb), 1))
    before = jnp.dot(jnp.where(tri, 1.0, 0.0).astype(BF16), a.astype(BF16), preferred_element_type=F32) + carry[...]
    e_out = jnp.zeros((tb, LANE), F32)
    r_out = jnp.zeros((tb, LANE), F32)
    g_out = jnp.zeros((tb, LANE), F32)
    for k in range(TOP_K):
        rank = jnp.sum(jnp.where(hot[k], before, 0.0), axis=-1, keepdims=True)
        e_out = jnp.where(lane == k, idxs[k], e_out)
        r_out = jnp.where(lane == k, rank, r_out)
        g_out = jnp.where(lane == k, jnp.where(valid, ex[k] / den, 0.0), g_out)
    e_ref[...] = e_out.astype(I32)
    r_ref[...] = r_out.astype(I32)
    g_ref[...] = g_out
    carry[...] = carry[...] + jnp.sum(a, axis=0, keepdims=True)
    cnt_ref[...] = carry[...]


def _route(logits, n_valid):
    mt = logits.shape[0]
    tb = _pick_tile(mt, (256, 128, 8))
    row = pl.BlockSpec((tb, LANE), lambda i: (i, 0))
    return pl.pallas_call(
        functools.partial(_route_kernel, tb=tb, n_valid=n_valid), grid=(mt // tb,),
        in_specs=[row], out_specs=[row, row, row, pl.BlockSpec((1, LANE), lambda i: (0, 0))],
        out_shape=[jax.ShapeDtypeStruct((mt, LANE), I32), jax.ShapeDtypeStruct((mt, LANE), I32),
                   jax.ShapeDtypeStruct((mt, LANE), F32), jax.ShapeDtypeStruct((1, LANE), F32)],
        scratch_shapes=[pltpu.VMEM((1, LANE), F32)],
        compiler_params=_cparams(("arbitrary",)), name="moe_route")(logits)


def _scatter_kernel(dest_ref, x_ref, xs_in, xs_out, sem, *, ts):
    del xs_in
    i = pl.program_id(0)

    def issue(t, c):
        for k in range(TOP_K):
            d = dest_ref[(i * ts + t) * TOP_K + k]
            pltpu.make_async_copy(x_ref.at[pl.ds(t, 1)], xs_out.at[pl.ds(d, 1)], sem).start()
        return c

    lax.fori_loop(0, ts, issue, 0)

    def drain(t, c):
        for k in range(TOP_K):
            pltpu.make_async_copy(x_ref.at[pl.ds(0, 1)], xs_out.at[pl.ds(0, 1)], sem).wait()
        return c

    lax.fori_loop(0, ts, drain, 0)


def _scatter_rows(dest_flat, x, xs_zero):
    mt, d = x.shape
    ts = _pick_tile(mt, (256, 128, 8))
    return pl.pallas_call(
        functools.partial(_scatter_kernel, ts=ts),
        grid_spec=pltpu.PrefetchScalarGridSpec(
            num_scalar_prefetch=1, grid=(mt // ts,),
            in_specs=[pl.BlockSpec((ts, d), lambda i, dest: (i, 0)), pl.BlockSpec(memory_space=pl.ANY)],
            out_specs=pl.BlockSpec(memory_space=pl.ANY),
            scratch_shapes=[pltpu.SemaphoreType.DMA]),
        out_shape=jax.ShapeDtypeStruct(xs_zero.shape, xs_zero.dtype),
        input_output_aliases={2: 0},
        compiler_params=_cparams(("arbitrary",)), name="moe_scatter")(dest_flat, x, xs_zero)


def _moe_up_kernel(be_ref, na_ref, x_ref, wg_ref, wl_ref, bg_ref, bl_ref, o_ref):
    del be_ref

    @pl.when(pl.program_id(1) < na_ref[0])
    def _():
        x = x_ref[...].astype(BF16)
        hg = jnp.dot(x, wg_ref[...], preferred_element_type=F32) + bg_ref[...]
        hl = jnp.dot(x, wl_ref[...], preferred_element_type=F32) + bl_ref[...]
        glu = jnp.minimum(hg, SWIGLU_LIMIT)
        lin = jnp.clip(hl, -SWIGLU_LIMIT, SWIGLU_LIMIT)
        o_ref[...] = (glu * jax.nn.sigmoid(SWIGLU_ALPHA * glu) * (lin + 1.0)).astype(BF16)

    @pl.when(pl.program_id(1) >= na_ref[0])
    def _():
        o_ref[...] = jnp.zeros_like(o_ref)


def _moe_up(block_e, n_act, xs, wg, wl, bg, bl, layer, n_blocks, tme):
    d = xs.shape[1]
    f = wg.shape[-1]
    tn = _pick_tile(f, (1024, 512, 256, 128))
    rows = lambda j, b, be, na: (jnp.minimum(b, na[0] - 1), 0)
    wspec = pl.BlockSpec((None, None, d, tn), lambda j, b, be, na: (layer, be[b], 0, j))
    bspec = pl.BlockSpec((None, None, 1, tn), lambda j, b, be, na: (layer, be[b], 0, j))
    return pl.pallas_call(
        _moe_up_kernel,
        grid_spec=pltpu.PrefetchScalarGridSpec(
            num_scalar_prefetch=2, grid=(f // tn, n_blocks),
            in_specs=[pl.BlockSpec((tme, d), rows), wspec, wspec, bspec, bspec],
            out_specs=pl.BlockSpec((tme, tn), lambda j, b, be, na: (b, j))),
        out_shape=jax.ShapeDtypeStruct((n_blocks * tme, f), BF16),
        compiler_params=_cparams(("arbitrary", "arbitrary"), 48), name="moe_up")(block_e, n_act, xs, wg, wl, bg, bl)


def _moe_down_kernel(be_ref, na_ref, a_ref, w_ref, b_ref, o_ref):
    del be_ref

    @pl.when(pl.program_id(1) < na_ref[0])
    def _():
        o_ref[...] = jnp.dot(a_ref[...], w_ref[...], preferred_element_type=F32) + b_ref[...]

    @pl.when(pl.program_id(1) >= na_ref[0])
    def _():
        o_ref[...] = jnp.zeros_like(o_ref)


def _moe_down(block_e, n_act, act, wd, bd, layer, n_blocks, tme):
    f = act.shape[1]
    d = wd.shape[-1]
    tn = _pick_tile(d, (1024, 512, 256, 128))
    rows = lambda j, b, be, na: (jnp.minimum(b, na[0] - 1), 0)
    return pl.pallas_call(
        _moe_down_kernel,
        grid_spec=pltpu.PrefetchScalarGridSpec(
            num_scalar_prefetch=2, grid=(d // tn, n_blocks),
            in_specs=[pl.BlockSpec((tme, f), rows),
                      pl.BlockSpec((None, None, f, tn), lambda j, b, be, na: (layer, be[b], 0, j)),
                      pl.BlockSpec((None, None, 1, tn), lambda j, b, be, na: (layer, be[b], 0, j))],
            out_specs=pl.BlockSpec((tme, tn), lambda j, b, be, na: (b, j))),
        out_shape=jax.ShapeDtypeStruct((n_blocks * tme, d), F32),
        compiler_params=_cparams(("arbitrary", "arbitrary"), 48), name="moe_down")(block_e, n_act, act, wd, bd)


def _combine_kernel(dest_ref, y_hbm, x_ref, gate_ref, g_ref, b_ref, xf_ref, xb_ref, buf, sem, *, tc, alpha):
    i = pl.program_id(0)

    def issue(t, c):
        for k in range(TOP_K):
            d = dest_ref[(i * tc + t) * TOP_K + k]
            pltpu.make_async_copy(y_hbm.at[pl.ds(d, 1)], buf.at[k, pl.ds(t, 1)], sem).start()
        return c

    lax.fori_loop(0, tc, issue, 0)

    def drain(t, c):
        for k in range(TOP_K):
            pltpu.make_async_copy(y_hbm.at[pl.ds(0, 1)], buf.at[k, pl.ds(0, 1)], sem).wait()
        return c

    lax.fori_loop(0, tc, drain, 0)
    gate = gate_ref[...]
    y = gate[:, 0:1] * buf[0]
    for k in range(1, TOP_K):
        y = y + gate[:, k:k + 1] * buf[k]
    out = _ln_rows(alpha * x_ref[...] + y, g_ref[...], b_ref[...])
    xf_ref[...] = out
    xb_ref[...] = out.astype(BF16)


def _combine(dest_flat, y_rows, x1, gates, g, b, layer, alpha):
    mt, d = x1.shape
    tc = _pick_tile(mt, (128, 8))
    row = lambda w: pl.BlockSpec((tc, w), lambda i, dest: (i, 0))
    vec = pl.BlockSpec((None, 1, d), lambda i, dest: (layer, 0, 0))
    return pl.pallas_call(
        functools.partial(_combine_kernel, tc=tc, alpha=alpha),
        grid_spec=pltpu.PrefetchScalarGridSpec(
            num_scalar_prefetch=1, grid=(mt // tc,),
            in_specs=[pl.BlockSpec(memory_space=pl.ANY), row(d), row(LANE), vec, vec],
            out_specs=[row(d), row(d)],
            scratch_shapes=[pltpu.VMEM((TOP_K, tc, d), F32), pltpu.SemaphoreType.DMA]),
        out_shape=[jax.ShapeDtypeStruct((mt, d), F32), jax.ShapeDtypeStruct((mt, d), BF16)],
        compiler_params=_cparams(("arbitrary",), 32), name="moe_combine_ln")(dest_flat, y_rows, x1, gates, g, b)


def _moe_layer(x1, logits, n_valid, mw, ln_g, ln_b, layer, alpha):
    mt, d = x1.shape
    tme = 256
    n_blocks = -(-(n_valid * TOP_K) // tme) + N_EXPERTS
    e_i, r_i, gates, counts = _route(logits, n_valid)
    cnt = counts[0, :N_EXPERTS].astype(I32)
    padded = (cnt + tme - 1) // tme * tme
    pend = jnp.cumsum(padded)
    pstart = pend - padded
    n_act = (pend[-1] // tme).astype(I32)
    blk = jnp.arange(n_blocks, dtype=I32)
    block_e = jnp.minimum(jnp.sum((blk[:, None] * tme >= pend[None, :]).astype(I32), axis=1), N_EXPERTS - 1)
    block_e = jnp.where(blk < n_act, block_e, block_e[jnp.maximum(n_act - 1, 0)]).astype(I32)
    e4 = e_i[:, :TOP_K]
    dest = pstart[jnp.clip(e4, 0, N_EXPERTS - 1)] + r_i[:, :TOP_K]
    tok = jnp.arange(mt, dtype=I32)[:, None]
    valid = tok < n_valid
    trash = n_blocks * tme + (tok - n_valid) * TOP_K + jnp.arange(TOP_K, dtype=I32)[None, :]
    dest_scatter = jnp.where(valid, dest, trash).astype(I32).reshape(-1)
    dest_gather = jnp.where(valid, dest, 0).astype(I32).reshape(-1)
    r_tot = n_blocks * tme + (mt - n_valid) * TOP_K
    r_tot = -(-r_tot // 8) * 8
    xs = _scatter_rows(dest_scatter, x1, jnp.zeros((r_tot, d), F32))
    n_act1 = n_act.reshape(1)
    act = _moe_up(block_e, n_act1, xs, mw["wg"], mw["wl"], mw["bg"], mw["bl"], layer, n_blocks, tme)
    y_rows = _moe_down(block_e, n_act1, act, mw["wd"], mw["bd"], layer, n_blocks, tme)
    return _combine(dest_gather, y_rows, x1, gates, ln_g, ln_b, layer, alpha)


def _flash_step(qb, kj, vj, mbias_list, m, l, acc):
    tq = mbias_list[0].shape[0]
    s = lax.dot_general(qb, kj, NT_DIMS, preferred_element_type=F32)
    sm = jnp.concatenate([s[r * tq:(r + 1) * tq] + mb for r, mb in enumerate(mbias_list)], axis=0)
    m_new = jnp.maximum(m, jnp.max(sm, axis=-1, keepdims=True))
    a = jnp.exp(m - m_new)
    p = jnp.exp(sm - m_new)
    l = a * l + jnp.sum(p, axis=-1, keepdims=True)
    acc = a * acc + jnp.dot(p.astype(BF16), vj, preferred_element_type=F32)
    return m_new, l, acc


def _nsa_kernel(q_ref, ck_ref, cv_ref, sk_ref, sv_ref, wk_ref, wv_ref, gt_ref, wpk_ref, wpv_ref, e_ref,
                o_ref, kcmp_s, vcmp_s, skb, svb, wkb, wvb, km_s, *, t_len, tq):
    g = pl.program_id(1)
    i = pl.program_id(2)
    nb = t_len // NSA_BLOCK
    tk = tq
    n_sel = min(N_SELECT, -(-t_len // NSA_BLOCK))

    @pl.when(i == 0)
    def _():
        ck = ck_ref[...].reshape(nb, NSA_BLOCK, HEAD_DIM)
        kcmp_s[...] = jnp.sum(ck * wpk_ref[...][None, :, :], axis=1).astype(BF16)
        cv = cv_ref[...].reshape(nb, NSA_BLOCK, HEAD_DIM)
        vcmp_s[...] = jnp.sum(cv * wpv_ref[...][None, :, :], axis=1).astype(BF16)
        skb[...] = sk_ref[...].astype(BF16)
        svb[...] = sv_ref[...].astype(BF16)
        wkb[...] = wk_ref[...].astype(BF16)
        wvb[...] = wv_ref[...].astype(BF16)

    s0 = i * tq
    slopes = [jnp.where(g == 0, float(SL_A[0][r]), float(SL_A[1][r])) for r in range(A_REP)]
    q = q_ref[...]
    q3b = (jnp.concatenate([q[:, r * HEAD_DIM:(r + 1) * HEAD_DIM] for r in range(A_REP)], axis=0)
           * (HEAD_DIM ** -0.5)).astype(BF16)
    qpos = s0 + lax.broadcasted_iota(I32, (tq, 1), 0)

    sc = lax.dot_general(q3b, kcmp_s[...], NT_DIMS, preferred_element_type=F32)
    blk = lax.broadcasted_iota(I32, (tq, nb), 1)
    endj = blk * NSA_BLOCK + (NSA_BLOCK - 1)
    okc = endj <= qpos
    distc = (qpos - endj).astype(F32)
    imp = jnp.zeros((tq, nb), F32)
    o_cmp = []
    for r in range(A_REP):
        sm = jnp.where(okc, sc[r * tq:(r + 1) * tq] - slopes[r] * distc, NEG)
        m = jnp.max(sm, axis=-1, keepdims=True)
        e = jnp.where(okc, jnp.exp(sm - m), 0.0)
        p = e / jnp.maximum(jnp.sum(e, axis=-1, keepdims=True), 1e-30)
        imp = imp + p
        o_cmp.append(jnp.dot(p.astype(BF16), vcmp_s[...], preferred_element_type=F32))

    cur = lax.shift_right_logical(qpos, int(math.log2(NSA_BLOCK)))
    score = jnp.where(blk == cur, jnp.inf, jnp.where(blk > cur, -jnp.inf, imp))
    cnt = jnp.zeros((tq, nb), F32)
    for c in range(nb):
        col = score[:, c:c + 1]
        beats = (col > score) | ((col == score) & (blk > c))
        cnt = cnt + jnp.where(beats, 1.0, 0.0)
    sel = jnp.where(cnt < n_sel, 1.0, 0.0).astype(BF16)
    km = jnp.dot(sel, e_ref[...], preferred_element_type=F32)
    for c in range(t_len // tk):
        km_s[c] = km[:, c * tk:(c + 1) * tk]

    rc = (lax.broadcasted_iota(I32, (tq, tk), 0) - lax.broadcasted_iota(I32, (tq, tk), 1)).astype(F32)
    m0 = jnp.full((A_REP * tq, 1), NEG, F32)
    l0 = jnp.zeros((A_REP * tq, 1), F32)
    a0 = jnp.zeros((A_REP * tq, HEAD_DIM), F32)

    def slc_body(j, carry):
        off = pl.multiple_of(j * tk, tk)
        dist = rc + (s0 - j * tk).astype(F32)
        ok = (dist >= 0.0) & (km_s[j] > 0.5)
        mb = [jnp.where(ok, -slopes[r] * dist, NEG) for r in range(A_REP)]
        return _flash_step(q3b, skb[pl.ds(off, tk), :], svb[pl.ds(off, tk), :], mb, *carry)

    _, l_s, acc_s = lax.fori_loop(0, i + 1, slc_body, (m0, l0, a0))
    o_slc = acc_s / l_s

    def win_body(j, carry):
        off = pl.multiple_of(j * tk, tk)
        dist = rc + (s0 - j * tk).astype(F32)
        ok = (dist >= 0.0) & (dist < float(WINDOW))
        mb = [jnp.where(ok, -slopes[r] * dist, NEG) for r in range(A_REP)]
        return _flash_step(q3b, wkb[pl.ds(off, tk), :], wvb[pl.ds(off, tk), :], mb, *carry)

    j_lo = jnp.maximum(s0 - (WINDOW - 1), 0) // tk
    _, l_w, acc_w = lax.fori_loop(j_lo, i + 1, win_body, (m0, l0, a0))
    o_win = acc_w / l_w

    gt = jax.nn.sigmoid(gt_ref[...])
    for r in range(A_REP):
        c = []
        for n in range(N_GATES):
            i0 = (0 * A_REP + r) * N_GATES + n
            i1 = (1 * A_REP + r) * N_GATES + n
            c.append(jnp.where(g == 0, gt[:, i0:i0 + 1], gt[:, i1:i1 + 1]))
        sl = slice(r * tq, (r + 1) * tq)
        out = c[0] * o_cmp[r] + c[1] * o_slc[sl] + c[2] * o_win[sl]
        o_ref[:, r * HEAD_DIM:(r + 1) * HEAD_DIM] = out.astype(BF16)


def _nsa_prompt(proj, cmp_wk, cmp_wv, e_mat, layer, n_b, t_len):
    tq = 128
    nq = t_len // tq
    nb = t_len // NSA_BLOCK
    gw = A_REP * HEAD_DIM
    seq = lambda cb: pl.BlockSpec((t_len, HEAD_DIM), lambda b, g, i, cb=cb: (b, cb + g))
    wp = pl.BlockSpec((None, NSA_BLOCK, HEAD_DIM), lambda b, g, i: (layer, 0, 0))
    return pl.pallas_call(
        functools.partial(_nsa_kernel, t_len=t_len, tq=tq), grid=(n_b, A_KV, nq),
        in_specs=[pl.BlockSpec((tq, gw), lambda b, g, i: (b * nq + i, g)),
                  seq(CB_NSA), seq(CB_NSA + 2), seq(CB_NSA + 4), seq(CB_NSA + 6), seq(CB_WIN), seq(CB_WIN + 2),
                  pl.BlockSpec((tq, LANE), lambda b, g, i: (b * nq + i, CB_GATE)),
                  wp, wp, pl.BlockSpec((nb, t_len), lambda b, g, i: (0, 0))],
        out_specs=pl.BlockSpec((tq, gw), lambda b, g, i: (b * nq + i, g)),
        out_shape=jax.ShapeDtypeStruct((n_b * t_len, A_HEADS * HEAD_DIM), BF16),
        scratch_shapes=[pltpu.VMEM((nb, HEAD_DIM), BF16), pltpu.VMEM((nb, HEAD_DIM), BF16),
                        pltpu.VMEM((t_len, HEAD_DIM), BF16), pltpu.VMEM((t_len, HEAD_DIM), BF16),
                        pltpu.VMEM((t_len, HEAD_DIM), BF16), pltpu.VMEM((t_len, HEAD_DIM), BF16),
                        pltpu.VMEM((t_len // tq, tq, tq), F32)],
        compiler_params=_cparams(("arbitrary", "arbitrary", "arbitrary"), 56), name="nsa_prompt")(
            proj, proj, proj, proj, proj, proj, proj, proj, cmp_wk, cmp_wv, e_mat)


def _diff_kernel(lam_ref, q_ref, k_ref, v_ref, g_ref, o_ref, kb, vb, *, tq, lam_init):
    h = pl.program_id(1)
    i = pl.program_id(2)
    tk = tq

    @pl.when(i == 0)
    def _():
        kb[...] = k_ref[...].astype(BF16)
        vb[...] = v_ref[...].astype(BF16)

    s0 = i * tq
    slope = _select_by_index(h, [float(s) for s in SL_B])
    q = q_ref[...] * (B_HALF ** -0.5)
    lane = lax.broadcasted_iota(I32, (tq, HEAD_DIM), 1)
    qs = jnp.concatenate([jnp.where(lane < B_HALF, q, 0.0), jnp.where(lane >= B_HALF, q, 0.0)], axis=0).astype(BF16)
    rc = (lax.broadcasted_iota(I32, (tq, tk), 0) - lax.broadcasted_iota(I32, (tq, tk), 1)).astype(F32)

    def body(j, carry):
        off = pl.multiple_of(j * tk, tk)
        dist = rc + (s0 - j * tk).astype(F32)
        mb = jnp.where(dist >= 0.0, -slope * dist, NEG)
        return _flash_step(qs, kb[pl.ds(off, tk), :], vb[pl.ds(off, tk), :], [mb, mb], *carry)

    m0 = jnp.full((2 * tq, 1), NEG, F32)
    l0 = jnp.zeros((2 * tq, 1), F32)
    a0 = jnp.zeros((2 * tq, HEAD_DIM), F32)
    _, l, acc = lax.fori_loop(0, i + 1, body, (m0, l0, a0))
    o = acc / l
    o = o[:tq] - lam_ref[0] * o[tq:]
    o = o * lax.rsqrt(jnp.mean(o * o, axis=-1, keepdims=True) + RMS_EPS) * g_ref[...] * (1.0 - lam_init)
    o_ref[...] = o.astype(BF16)


def _diff_prompt(lam, proj, diff_g, layer, n_b, t_len, lam_init):
    tq = 128
    nq = t_len // tq
    seq = lambda cb: pl.BlockSpec((t_len, HEAD_DIM), lambda b, h, i, cb=cb: (b, cb + h))
    return pl.pallas_call(
        functools.partial(_diff_kernel, tq=tq, lam_init=lam_init), grid=(n_b, B_HEADS, nq),
        in_specs=[pl.BlockSpec(memory_space=pltpu.SMEM),
                  pl.BlockSpec((tq, HEAD_DIM), lambda b, h, i: (b * nq + i, CB_BQ + h)),
                  seq(CB_BK), seq(CB_BV),
                  pl.BlockSpec((None, 1, HEAD_DIM), lambda b, h, i: (layer, 0, 0))],
        out_specs=pl.BlockSpec((tq, HEAD_DIM), lambda b, h, i: (b * nq + i, h)),
        out_shape=jax.ShapeDtypeStruct((n_b * t_len, B_HEADS * HEAD_DIM), BF16),
        scratch_shapes=[pltpu.VMEM((t_len, HEAD_DIM), BF16), pltpu.VMEM((t_len, HEAD_DIM), BF16)],
        compiler_params=_cparams(("arbitrary", "arbitrary", "arbitrary"), 32), name="diff_prompt")(
            lam, proj, proj, proj, diff_g)


def _sgu_kernel(u_ref, v_ref, g_ref, b_ref, w_ref, bt_ref, o_ref, *, n_chunks):
    u = jax.nn.gelu(u_ref[...])
    v = _ln_rows(jax.nn.gelu(v_ref[...]), g_ref[...], b_ref[...])
    tri = lax.broadcasted_iota(I32, (CHUNK, CHUNK), 0) >= lax.broadcasted_iota(I32, (CHUNK, CHUNK), 1)
    for gi in range(C_GROUPS):
        w = jnp.where(tri, w_ref[gi], 0.0).astype(BF16)
        cols = slice(gi * HEAD_DIM, (gi + 1) * HEAD_DIM)
        for c in range(n_chunks):
            rows = slice(c * CHUNK, (c + 1) * CHUNK)
            s = jnp.dot(w, v[rows, cols].astype(BF16), preferred_element_type=F32) + bt_ref[:, gi:gi + 1]
            o_ref[rows, cols] = (u[rows, cols] * s).astype(BF16)


def _sgu_prompt(proj, ln_g, ln_b, sgu_w, sgu_bt, layer, m_p):
    n_chunks = _pick_tile(m_p // CHUNK, (4, 2, 1))
    tr = n_chunks * CHUNK
    vec = pl.BlockSpec((None, 1, C_WIDTH), lambda i: (layer, 0, 0))
    return pl.pallas_call(
        functools.partial(_sgu_kernel, n_chunks=n_chunks), grid=(m_p // tr,),
        in_specs=[pl.BlockSpec((tr, C_WIDTH), lambda i: (i, CB_CU // C_GROUPS)),
                  pl.BlockSpec((tr, C_WIDTH), lambda i: (i, CB_CV // C_GROUPS)),
                  vec, vec,
                  pl.BlockSpec((None, C_GROUPS, CHUNK, CHUNK), lambda i: (layer, 0, 0, 0)),
                  pl.BlockSpec((None, CHUNK, C_GROUPS), lambda i: (layer, 0, 0))],
        out_specs=pl.BlockSpec((tr, C_WIDTH), lambda i: (i, 0)),
        out_shape=jax.ShapeDtypeStruct((m_p, C_WIDTH), BF16),
        compiler_params=_cparams(("arbitrary",)), name="sgu_prompt")(proj, proj, ln_g, ln_b, sgu_w, sgu_bt)


def _page_specs(n_pages_step, block, col_block, layer):
    return [pl.BlockSpec((None, None) + block,
                         lambda b, c, pt, i=i: (layer, pt[b, c * n_pages_step + i], 0, col_block))
            for i in range(n_pages_step)]


def _pool_kernel(pt_ref, *refs, pg):
    del pt_ref
    pages, w_ref, o_ref = refs[:pg], refs[pg], refs[pg + 1]
    w = w_ref[...]
    per_page = pages[0].shape[0] // NSA_BLOCK
    for i in range(pg):
        x = pages[i][...]
        for hb in range(per_page):
            r = i * per_page + hb
            o_ref[r:r + 1, :] = jnp.sum(x[hb * NSA_BLOCK:(hb + 1) * NSA_BLOCK] * w, axis=0, keepdims=True)


def _sample_pool(page_table, cache_nsa4, w_pool, layer, pg):
    n_s, n_pages = page_table.shape
    page = cache_nsa4.shape[2]
    half = 2 * A_KV * HEAD_DIM
    per_page = page // NSA_BLOCK
    nc = n_pages // pg
    return pl.pallas_call(
        functools.partial(_pool_kernel, pg=pg),
        grid_spec=pltpu.PrefetchScalarGridSpec(
            num_scalar_prefetch=1, grid=(n_s, nc),
            in_specs=_page_specs(pg, (page, half), 0, layer) + [pl.BlockSpec((NSA_BLOCK, half), lambda b, c, pt: (0, 0))],
            out_specs=pl.BlockSpec((None, pg * per_page, half), lambda b, c, pt: (b, c, 0))),
        out_shape=jax.ShapeDtypeStruct((n_s, n_pages * per_page, half), F32),
        compiler_params=_cparams(("arbitrary", "arbitrary"), 32), name="sample_cmp_pool")(
            page_table, *([cache_nsa4] * pg), w_pool)


def _q3_rows(prow, g):
    rowi = lax.broadcasted_iota(I32, (8, HEAD_DIM), 0)
    q3 = jnp.zeros((8, HEAD_DIM), F32)
    for r in range(A_REP):
        c0 = (CB_AQ + g * A_REP + r) * HEAD_DIM
        q3 = jnp.where(rowi == r, jnp.broadcast_to(prow[:, c0:c0 + HEAD_DIM], (8, HEAD_DIM)), q3)
    return q3 * (HEAD_DIM ** -0.5)


def _select_kernel(p_ref, kv_ref, sl_ref, e_ref, ocmp_ref, km_ref, *, p_len, nbp, bpc):
    prow = p_ref[...]
    n_blocks = nbp + 1
    n_sel = min(N_SELECT, n_blocks)
    blk = lax.broadcasted_iota(I32, (8, nbp), 1)
    dist = (p_len - (blk * NSA_BLOCK + NSA_BLOCK - 1)).astype(F32)
    lane_f = lax.broadcasted_iota(I32, (1, nbp), 1).astype(F32)
    rowi = lax.broadcasted_iota(I32, (8, nbp), 0)
    sel_rows = jnp.zeros((8, nbp), F32)
    outs = []
    for g in range(A_KV):
        q3 = _q3_rows(prow, g).astype(BF16)
        kc = kv_ref[:, g * HEAD_DIM:(g + 1) * HEAD_DIM].astype(BF16)
        vc = kv_ref[:, (A_KV + g) * HEAD_DIM:(A_KV + g + 1) * HEAD_DIM].astype(BF16)
        s = lax.dot_general(q3, kc, NT_DIMS, preferred_element_type=F32) - sl_ref[g] * dist
        m = jnp.max(s, axis=-1, keepdims=True)
        e = jnp.exp(s - m)
        p = e / jnp.maximum(jnp.sum(e, axis=-1, keepdims=True), 1e-30)
        o = jnp.dot(p.astype(BF16), vc, preferred_element_type=F32)
        outs += [o[r:r + 1] for r in range(A_REP)]
        work = p[0:1] + p[1:2] + p[2:3]
        sel = jnp.zeros((1, nbp), F32)
        for _ in range(n_sel - 1):
            mx = jnp.max(work, axis=-1, keepdims=True)
            idx = jnp.min(jnp.where(work == mx, lane_f, float(nbp)), axis=-1, keepdims=True)
            hit = lane_f == idx
            sel = jnp.where(hit, 1.0, sel)
            work = jnp.where(hit, -jnp.inf, work)
        sel_rows = jnp.where(rowi == g, sel, sel_rows)
    ocmp_ref[...] = jnp.concatenate(outs, axis=1)
    kpc = bpc * NSA_BLOCK
    for c in range(nbp // bpc):
        in_chunk = (blk >= c * bpc) & (blk < (c + 1) * bpc)
        km_ref[:, c * kpc:(c + 1) * kpc] = jnp.dot(jnp.where(in_chunk, sel_rows, 0.0).astype(BF16), e_ref[...],
                                                   preferred_element_type=F32)


def _sample_select(prow, kvcmp, sl_cols, e_small, p_len, bpc):
    n_s = prow.shape[0]
    nbp = kvcmp.shape[1]
    half = kvcmp.shape[2]
    return pl.pallas_call(
        functools.partial(_select_kernel, p_len=p_len, nbp=nbp, bpc=bpc), grid=(n_s,),
        in_specs=[pl.BlockSpec((None, 1, N_PROJ), lambda b: (b, 0, 0)),
                  pl.BlockSpec((None, nbp, half), lambda b: (b, 0, 0)),
                  pl.BlockSpec((A_KV, 8, 1), lambda b: (0, 0, 0)),
                  pl.BlockSpec((nbp, bpc * NSA_BLOCK), lambda b: (0, 0))],
        out_specs=[pl.BlockSpec((None, 1, A_HEADS * HEAD_DIM), lambda b: (b, 0, 0)),
                   pl.BlockSpec((None, 8, p_len), lambda b: (b, 0, 0))],
        out_shape=[jax.ShapeDtypeStruct((n_s, 1, A_HEADS * HEAD_DIM), F32),
                   jax.ShapeDtypeStruct((n_s, 8, p_len), F32)],
        compiler_params=_cparams(("arbitrary",), 32), name="sample_select")(prow, kvcmp, sl_cols, e_small)


def _decode_update(s, ok, vb_list, m_ref, l_ref, acc_ref, page):
    sm = s if ok is None else jnp.where(ok, s, NEG)
    m_old = m_ref[...]
    m_new = jnp.maximum(m_old, jnp.max(sm, axis=-1, keepdims=True))
    a = jnp.exp(m_old - m_new)
    p = jnp.exp(sm - m_new)
    if ok is not None:
        p = jnp.where(ok, p, 0.0)
    l_ref[...] = a * l_ref[...] + jnp.sum(p, axis=-1, keepdims=True)
    pb = p.astype(BF16)
    pv = jnp.dot(pb[:, 0:page], vb_list[0], preferred_element_type=F32)
    for i in range(1, len(vb_list)):
        pv = pv + jnp.dot(pb[:, i * page:(i + 1) * page], vb_list[i], preferred_element_type=F32)
    acc_ref[...] = a * acc_ref[...] + pv
    m_ref[...] = m_new


def _nsa_decode_kernel(pt_ref, *refs, pg, p_len, n_win):
    del pt_ref
    pages = refs[:pg]
    (p_ref, km_ref, win_ref, ocmp_ref, sl_ref, sg_ref, sb_ref, w00_ref, b0_ref,
     o_ref, osgu_ref, vrow_ref, m_s, l_s, acc_s) = refs[pg:]
    c = pl.program_id(1)
    page = pages[0].shape[0]
    gw = A_KV * HEAD_DIM

    @pl.when(c == 0)
    def _():
        m_s[...] = jnp.full_like(m_s, NEG)
        l_s[...] = jnp.zeros_like(l_s)
        acc_s[...] = jnp.zeros_like(acc_s)

    prow = p_ref[...]
    kpos = c * (pg * page) + lax.broadcasted_iota(I32, (8, pg * page), 1)
    dist = (p_len - kpos).astype(F32)
    xs = [pages[i][...] for i in range(pg)]
    for g in range(A_KV):
        q3 = _q3_rows(prow, g).astype(BF16)
        s = jnp.concatenate(
            [lax.dot_general(q3, x[:, g * HEAD_DIM:(g + 1) * HEAD_DIM].astype(BF16), NT_DIMS,
                             preferred_element_type=F32) for x in xs], axis=1)
        s = s - sl_ref[g] * dist
        ok = jnp.broadcast_to(km_ref[g:g + 1, :], s.shape) > 0.5
        vbs = [x[:, gw + g * HEAD_DIM:gw + (g + 1) * HEAD_DIM].astype(BF16) for x in xs]
        _decode_update(s, ok, vbs, m_s.at[g], l_s.at[g], acc_s.at[g], page)

    @pl.when(c == pl.num_programs(1) - 1)
    def _():
        gt = jax.nn.sigmoid(prow[:, CB_GATE * LANE:(CB_GATE + 1) * LANE])
        ocmp = ocmp_ref[...]
        wpos = p_len - n_win + lax.broadcasted_iota(I32, (8, n_win), 1)
        wd = (p_len - wpos).astype(F32)
        wok = (wd >= 0.0) & (wd < float(WINDOW))
        outs = []
        for g in range(A_KV):
            q3 = _q3_rows(prow, g)
            q3b = q3.astype(BF16)
            nsa0 = (CB_NSA + 2 * A_KV) * HEAD_DIM
            k_new = prow[:, nsa0 + g * HEAD_DIM:nsa0 + (g + 1) * HEAD_DIM]
            v_new = prow[:, nsa0 + gw + g * HEAD_DIM:nsa0 + gw + (g + 1) * HEAD_DIM]
            s_new = jnp.sum(q3b.astype(F32) * k_new.astype(BF16).astype(F32), axis=-1, keepdims=True)
            m_old = m_s[g]
            m_f = jnp.maximum(m_old, s_new)
            a = jnp.exp(m_old - m_f)
            pn = jnp.exp(s_new - m_f)
            o_slc = (a * acc_s[g] + pn * v_new.astype(BF16).astype(F32)) / (a * l_s[g] + pn)
            wk = win_ref[:, g * HEAD_DIM:(g + 1) * HEAD_DIM].astype(BF16)
            wv = win_ref[:, gw + g * HEAD_DIM:gw + (g + 1) * HEAD_DIM].astype(BF16)
            win0 = CB_WIN * HEAD_DIM
            wk_new = prow[:, win0 + g * HEAD_DIM:win0 + (g + 1) * HEAD_DIM]
            wv_new = prow[:, win0 + gw + g * HEAD_DIM:win0 + gw + (g + 1) * HEAD_DIM]
            sw = lax.dot_general(q3b, wk, NT_DIMS, preferred_element_type=F32) - sl_ref[g] * wd
            sw = jnp.where(wok, sw, NEG)
            sw_new = jnp.sum(q3b.astype(F32) * wk_new.astype(BF16).astype(F32), axis=-1, keepdims=True)
            mw = jnp.maximum(jnp.max(sw, axis=-1, keepdims=True), sw_new)
            pw = jnp.where(wok, jnp.exp(sw - mw), 0.0)
            pwn = jnp.exp(sw_new - mw)
            o_win = ((jnp.dot(pw.astype(BF16), wv, preferred_element_type=F32) + pwn * wv_new.astype(BF16).astype(F32))
                     / (jnp.sum(pw, axis=-1, keepdims=True) + pwn))
            for r in range(A_REP):
                hd = g * A_REP + r
                gi = hd * N_GATES
                oc = ocmp[:, hd * HEAD_DIM:(hd + 1) * HEAD_DIM]
                outs.append(gt[:, gi:gi + 1] * oc + gt[:, gi + 1:gi + 2] * o_slc[r:r + 1]
                            + gt[:, gi + 2:gi + 3] * o_win[r:r + 1])
        o_ref[...] = jnp.concatenate(outs, axis=1)
        u = jax.nn.gelu(prow[:, CB_CU * LANE:CB_CU * LANE + C_WIDTH])
        v = _ln_rows(jax.nn.gelu(prow[:, CB_CV * LANE:CB_CV * LANE + C_WIDTH]), sg_ref[...], sb_ref[...])
        vrow_ref[...] = v
        osgu_ref[...] = u * (w00_ref[...] * v + b0_ref[...])


def _sample_nsa(page_table, cache_nsa4, prow, km, win_l, ocmp, sl_cols, sgu_g, sgu_b, w00, b0, layer, pg, p_len):
    n_s, n_pages = page_table.shape
    page = cache_nsa4.shape[2]
    half = 2 * A_KV * HEAD_DIM
    n_win = win_l.shape[2]
    nc = n_pages // pg
    one = lambda w: pl.BlockSpec((None, 1, w), lambda b, c, pt: (b, 0, 0))
    lvec = pl.BlockSpec((None, 1, C_WIDTH), lambda b, c, pt: (layer, 0, 0))
    return pl.pallas_call(
        functools.partial(_nsa_decode_kernel, pg=pg, p_len=p_len, n_win=n_win),
        grid_spec=pltpu.PrefetchScalarGridSpec(
            num_scalar_prefetch=1, grid=(n_s, nc),
            in_specs=_page_specs(pg, (page, half), 1, layer) + [
                one(N_PROJ),
                pl.BlockSpec((None, 8, pg * page), lambda b, c, pt: (b, 0, c)),
                pl.BlockSpec((None, None, n_win, half), lambda b, c, pt: (layer, b, 0, 0)),
                one(A_HEADS * HEAD_DIM),
                pl.BlockSpec((A_KV, 8, 1), lambda b, c, pt: (0, 0, 0)),
                lvec, lvec, lvec, lvec],
            out_specs=[one(A_HEADS * HEAD_DIM), one(C_WIDTH), one(C_WIDTH)],
            scratch_shapes=[pltpu.VMEM((A_KV, 8, 1), F32), pltpu.VMEM((A_KV, 8, 1), F32),
                            pltpu.VMEM((A_KV, 8, HEAD_DIM), F32)]),
        out_shape=[jax.ShapeDtypeStruct((n_s, 1, A_HEADS * HEAD_DIM), F32),
                   jax.ShapeDtypeStruct((n_s, 1, C_WIDTH), F32), jax.ShapeDtypeStruct((n_s, 1, C_WIDTH), F32)],
        compiler_params=_cparams(("arbitrary", "arbitrary"), 40), name="sample_nsa")(
            page_table, *([cache_nsa4] * pg), prow, km, win_l, ocmp, sl_cols, sgu_g, sgu_b, w00, b0)


def _diff_decode_kernel(pt_ref, *refs, pg, p_len, lam_init):
    del pt_ref
    pages = refs[:pg]
    p_ref, sl_ref, g_ref, lam_ref, o_ref, m_s, l_s, acc_s = refs[pg:]
    c = pl.program_id(1)
    page = pages[0].shape[0]
    hw = B_HEADS * HEAD_DIM
    nr = 16

    @pl.when(c == 0)
    def _():
        m_s[...] = jnp.full_like(m_s, NEG)
        l_s[...] = jnp.zeros_like(l_s)
        acc_s[...] = jnp.zeros_like(acc_s)

    prow = p_ref[...]
    q = prow[:, CB_BQ * LANE:CB_BQ * LANE + hw] * (B_HALF ** -0.5)
    seg = lax.shift_right_logical(lax.broadcasted_iota(I32, (nr, hw), 1), int(math.log2(B_HALF)))
    rowi = lax.broadcasted_iota(I32, (nr, hw), 0)
    qbd = jnp.where(seg == rowi, jnp.broadcast_to(q, (nr, hw)), 0.0)
    qbd_b = qbd.astype(BF16)
    kpos = c * (pg * page) + lax.broadcasted_iota(I32, (nr, pg * page), 1)
    dist = (p_len - kpos).astype(F32)
    xs = [pages[i][...] for i in range(pg)]
    s = jnp.concatenate([lax.dot_general(qbd_b, x[:, 0:hw].astype(BF16), NT_DIMS, preferred_element_type=F32)
                         for x in xs], axis=1)
    s = s - sl_ref[...] * dist
    vbs = [x[:, hw:2 * hw].astype(BF16) for x in xs]
    _decode_update(s, None, vbs, m_s, l_s, acc_s, page)

    @pl.when(c == pl.num_programs(1) - 1)
    def _():
        k_new = prow[:, CB_BK * LANE:CB_BK * LANE + hw].astype(BF16).astype(F32)
        v_new = prow[:, CB_BV * LANE:CB_BV * LANE + hw].astype(BF16).astype(F32)
        s_new = jnp.sum(qbd_b.astype(F32) * k_new, axis=-1, keepdims=True)
        m_old = m_s[...]
        m_f = jnp.maximum(m_old, s_new)
        a = jnp.exp(m_old - m_f)
        pn = jnp.exp(s_new - m_f)
        o = (a * acc_s[...] + pn * v_new) / (a * l_s[...] + pn)
        outs = []
        for h in range(B_HEADS):
            cols = slice(h * HEAD_DIM, (h + 1) * HEAD_DIM)
            oh = o[2 * h:2 * h + 1, cols] - lam_ref[0] * o[2 * h + 1:2 * h + 2, cols]
            oh = oh * lax.rsqrt(jnp.mean(oh * oh, axis=-1, keepdims=True) + RMS_EPS) * g_ref[...] * (1.0 - lam_init)
            outs.append(oh)
        o_ref[...] = jnp.concatenate(outs, axis=1)


def _sample_diff(page_table, lam, cache_diff4, prow, sl_col, diff_g, layer, pg, p_len, lam_init):
    n_s, n_pages = page_table.shape
    page = cache_diff4.shape[2]
    hw = B_HEADS * HEAD_DIM
    nc = n_pages // pg
    return pl.pallas_call(
        functools.partial(_diff_decode_kernel, pg=pg, p_len=p_len, lam_init=lam_init),
        grid_spec=pltpu.PrefetchScalarGridSpec(
            num_scalar_prefetch=1, grid=(n_s, nc),
            in_specs=[pl.BlockSpec((None, None, page, 2 * hw),
                                   lambda b, c, pt, i=i: (layer, pt[b, c * pg + i], 0, 0)) for i in range(pg)] + [
                pl.BlockSpec((None, 1, N_PROJ), lambda b, c, pt: (b, 0, 0)),
                pl.BlockSpec((16, 1), lambda b, c, pt: (0, 0)),
                pl.BlockSpec((None, 1, HEAD_DIM), lambda b, c, pt: (layer, 0, 0)),
                pl.BlockSpec(memory_space=pltpu.SMEM)],
            out_specs=pl.BlockSpec((None, 1, hw), lambda b, c, pt: (b, 0, 0)),
            scratch_shapes=[pltpu.VMEM((16, 1), F32), pltpu.VMEM((16, 1), F32), pltpu.VMEM((16, hw), F32)]),
        out_shape=jax.ShapeDtypeStruct((n_s, 1, hw), F32),
        compiler_params=_cparams(("arbitrary", "arbitrary"), 40), name="sample_diff")(
            page_table, *([cache_diff4] * pg), prow, sl_col, diff_g, lam)


def kernel(x_prompt, x_sample, cache_nsa, cache_diff, state_win, page_table, ln_in_g, ln_in_b, w_in, cmp_wk, cmp_wv,
           lam_q1, lam_k1, lam_q2, lam_k2, diff_norm_g, sgu_ln_g, sgu_ln_b, sgu_w, sgu_b, w_out, ln1_g, ln1_b,
           router_w, router_b, w_up, b_up, w_down, b_down, ln2_g, ln2_b):
    n_b, t_len, d = x_prompt.shape
    n_s, t_s, _ = x_sample.shape
    depth = w_in.shape[0]
    assert t_s == 1 and t_len % CHUNK == 0
    n_pages = page_table.shape[1]
    page = cache_nsa.shape[2]
    p_len = n_pages * page
    n_win = state_win.shape[2]
    assert p_len % CHUNK == 0 and page % NSA_BLOCK == 0 and n_win == min(WINDOW, p_len)
    m_p = n_b * t_len
    n_valid = m_p + n_s
    mt = -(-n_valid // 256) * 256
    alpha = (2 * depth) ** 0.25
    pg = _pick_tile(n_pages, (8, 4, 2, 1))
    bpc = pg * (page // NSA_BLOCK)

    w_in_r = jnp.concatenate(
        [w_in[:, :, :GATE_ORIG], w_in[:, :, GATE_ORIG + N_GATE_COLS:], w_in[:, :, GATE_ORIG:GATE_ORIG + N_GATE_COLS],
         jnp.zeros((depth, d, LANE - N_GATE_COLS), w_in.dtype)], axis=-1).astype(BF16)
    assert w_in_r.shape[-1] == N_PROJ
    w_out_b = w_out.astype(BF16)
    mw = {"wg": w_up[..., 0::2].astype(BF16), "wl": w_up[..., 1::2].astype(BF16),
          "bg": b_up[..., 0::2][:, :, None, :], "bl": b_up[..., 1::2][:, :, None, :],
          "wd": w_down.astype(BF16), "bd": b_down[:, :, None, :]}
    rw_pad = jnp.pad(router_w, ((0, 0), (0, 0), (0, LANE - N_EXPERTS)))
    rb_pad = jnp.pad(router_b, ((0, 0), (0, LANE - N_EXPERTS)), constant_values=NEG)[:, None, :]
    ln1_g3, ln1_b3, ln2_g3, ln2_b3 = (a[:, None, :] for a in (ln1_g, ln1_b, ln2_g, ln2_b))
    sgu_g3, sgu_b3 = sgu_ln_g[:, None, :], sgu_ln_b[:, None, :]
    sgu_bt = jnp.swapaxes(sgu_b, 1, 2)
    sgu_w00 = jnp.repeat(sgu_w[:, :, 0, 0], HEAD_DIM, axis=1)[:, None, :]
    sgu_b0 = jnp.repeat(sgu_b[:, :, 0], HEAD_DIM, axis=1)[:, None, :]
    diff_g3 = diff_norm_g[:, None, :]
    nb_p = t_len // NSA_BLOCK
    e_prompt = jnp.asarray((np.arange(t_len)[None, :] // NSA_BLOCK == np.arange(nb_p)[:, None]), BF16)
    e_small = jnp.asarray((np.arange(bpc * NSA_BLOCK)[None, :] // NSA_BLOCK == np.arange(p_len // NSA_BLOCK)[:, None] % bpc), BF16)
    sl_a_cols = jnp.asarray(np.concatenate([SL_A, np.zeros((A_KV, 8 - A_REP), np.float32)], axis=1)[:, :, None])
    sl_b_col = jnp.asarray(np.concatenate([np.repeat(SL_B, 2), np.zeros(16 - 2 * B_HEADS, np.float32)])[:, None])
    w_pool = jnp.concatenate([jnp.tile(cmp_wk, (1, 1, A_KV)), jnp.tile(cmp_wv, (1, 1, A_KV))], axis=-1)
    cache_nsa4 = cache_nsa.reshape(depth, cache_nsa.shape[1], page, -1)
    cache_diff4 = cache_diff.reshape(depth, cache_diff.shape[1], page, -1)
    win4 = state_win.reshape(depth, n_s, n_win, -1)

    x_all = jnp.concatenate([x_prompt.reshape(m_p, d), x_sample.reshape(n_s, d),
                             jnp.zeros((mt - n_valid, d), x_prompt.dtype)], axis=0)
    xf, xb = _ln_in(x_all, ln_in_g, ln_in_b)

    nsa_p, nsa_s, diff_p, diff_s, win_p, win_s, sgu_s = [], [], [], [], [], [], []
    for l in range(depth):
        lam_init = 0.8 - 0.6 * math.exp(-0.3 * l)
        lam = (jnp.exp(jnp.sum(lam_q1[l] * lam_k1[l])) - jnp.exp(jnp.sum(lam_q2[l] * lam_k2[l])) + lam_init)
        lam = lam.astype(F32).reshape(1)
        proj = _proj(xb, w_in_r, l)
        prow = proj[m_p:n_valid].reshape(n_s, 1, N_PROJ)

        o_nsa = _nsa_prompt(proj, cmp_wk, cmp_wv, e_prompt, l, n_b, t_len)
        o_diff = _diff_prompt(lam, proj, diff_g3, l, n_b, t_len, lam_init)
        o_sgu = _sgu_prompt(proj, sgu_g3, sgu_b3, sgu_w, sgu_bt, l, m_p)

        kvcmp = _sample_pool(page_table, cache_nsa4, w_pool[l], l, pg)
        ocmp_s, km = _sample_select(prow, kvcmp, sl_a_cols, e_small, p_len, bpc)
        o_nsa_s, o_sgu_s, v_rows = _sample_nsa(page_table, cache_nsa4, prow, km, win4, ocmp_s, sl_a_cols,
                                               sgu_g3, sgu_b3, sgu_w00, sgu_b0, l, pg, p_len)
        o_diff_s = _sample_diff(page_table, lam, cache_diff4, prow, sl_b_col, diff_g3, l, pg, p_len, lam_init)

        pad = lambda w: jnp.zeros((mt - n_valid, w), BF16)
        hn = jnp.concatenate([o_nsa, o_nsa_s.reshape(n_s, -1).astype(BF16), pad(o_nsa.shape[1])], axis=0)
        hd = jnp.concatenate([o_diff, o_diff_s.reshape(n_s, -1).astype(BF16), pad(o_diff.shape[1])], axis=0)
        hs = jnp.concatenate([o_sgu, o_sgu_s.reshape(n_s, -1).astype(BF16), pad(o_sgu.shape[1])], axis=0)
        x1, logits = _outproj(hn, hd, hs, w_out_b, xf, ln1_g3, ln1_b3, rw_pad, rb_pad, l, alpha)
        xf, xb = _moe_layer(x1, logits, n_valid, mw, ln2_g3, ln2_b3, l, alpha)

        c0, c1 = CB_NSA * LANE, CB_WIN * LANE
        nsa_p.append(proj[:m_p, c0:c1].reshape(n_b, t_len, 4, A_KV, HEAD_DIM))
        nsa_s.append(proj[m_p:n_valid, c0:c1].reshape(n_s, 1, 4, A_KV, HEAD_DIM))
        d0, d1 = CB_BK * LANE, CB_CU * LANE
        diff_p.append(proj[:m_p, d0:d1].reshape(n_b, t_len, 2, B_HEADS, HEAD_DIM))
        diff_s.append(proj[m_p:n_valid, d0:d1].reshape(n_s, 1, 2, B_HEADS, HEAD_DIM))
        w_rows = proj[:m_p, c1:CB_BQ * LANE].reshape(n_b, t_len, 2, A_KV, HEAD_DIM)
        win_p.append(w_rows[:, t_len - min(WINDOW, t_len):])
        w_new = proj[m_p:n_valid, c1:CB_BQ * LANE].reshape(n_s, 1, 2, A_KV, HEAD_DIM)
        win_s.append(jnp.concatenate([state_win[l], w_new], axis=1)[:, 1:])
        sgu_s.append(v_rows.reshape(n_s, 1, C_WIDTH))

    y_p = xf[:m_p].reshape(n_b, t_len, d)
    y_s = xf[m_p:n_valid].reshape(n_s, 1, d)
    return (y_p, y_s, jnp.stack(nsa_p), jnp.stack(nsa_s), jnp.stack(diff_p), jnp.stack(diff_s),
            jnp.stack(win_p), jnp.stack(win_s), jnp.stack(sgu_s))
```

```python
import functools
import math

import numpy as np
import jax
import jax.numpy as jnp
from jax import lax
from jax.experimental import pallas as pl
from jax.experimental.pallas import tpu as pltpu

F32 = jnp.float32
BF16 = jnp.bfloat16
I32 = jnp.int32

HEAD_DIM = 128
A_KV = 2
A_REP = 3
A_HEADS = A_KV * A_REP
B_HEADS = 6
B_HALF = HEAD_DIM // 2
C_GROUPS = 4
C_WIDTH = C_GROUPS * HEAD_DIM
CHUNK = 128
NSA_BLOCK = 64
N_SELECT = 16
WINDOW = 512
N_GATES = 3
N_EXPERTS = 32
TOP_K = 4
SWIGLU_LIMIT = 7.0
SWIGLU_ALPHA = 1.702
LN_EPS = 1e-5
RMS_EPS = 1e-5
LANE = 128
NEG = -1e30

_SL = (2.0 ** (-8.0 * np.arange(1, A_HEADS + B_HEADS + 1) / (A_HEADS + B_HEADS))).astype(np.float32)
SL_A = _SL[0::2].reshape(A_KV, A_REP)
SL_B = _SL[1::2]

CB_AQ = 0
CB_NSA = 6
CB_WIN = 14
CB_BQ = 18
CB_BK = 24
CB_BV = 30
CB_CU = 36
CB_CV = 40
CB_GATE = 44
N_PROJ = 45 * LANE
N_GATE_COLS = A_HEADS * N_GATES
GATE_ORIG = 2304

NT_DIMS = (((1,), (1,)), ((), ()))
PROMPT_TK = (512, 256, 128)


def _cparams(sem, vmem_mb=None):
    kw = dict(dimension_semantics=sem)
    if vmem_mb is not None:
        kw["vmem_limit_bytes"] = vmem_mb * 1024 * 1024
    return pltpu.CompilerParams(**kw)


def _pick_tile(n, candidates):
    for c in candidates:
        if n % c == 0:
            return c
    return n


def _ln_rows(x, g, b):
    mu = jnp.mean(x, axis=-1, keepdims=True)
    xc = x - mu
    var = jnp.mean(xc * xc, axis=-1, keepdims=True)
    return xc * lax.rsqrt(var + LN_EPS) * g + b


def _select_by_index(idx, values):
    out = values[-1]
    for j in range(len(values) - 2, -1, -1):
        out = jnp.where(idx == j, values[j], out)
    return out


def _ln_in_kernel(x_ref, g_ref, b_ref, of_ref, ob_ref):
    y = _ln_rows(x_ref[...], g_ref[...], b_ref[...])
    of_ref[...] = y
    ob_ref[...] = y.astype(BF16)


def _ln_in(x, g, b):
    mt, d = x.shape
    tm = _pick_tile(mt, (256, 128, 8))
    row = pl.BlockSpec((tm, d), lambda i: (i, 0))
    vec = pl.BlockSpec((1, d), lambda i: (0, 0))
    return pl.pallas_call(
        _ln_in_kernel, grid=(mt // tm,), in_specs=[row, vec, vec], out_specs=[row, row],
        out_shape=[jax.ShapeDtypeStruct((mt, d), F32), jax.ShapeDtypeStruct((mt, d), BF16)],
        compiler_params=_cparams(("arbitrary",)), name="ln_in")(x, g.reshape(1, d), b.reshape(1, d))


def _proj_kernel(x_ref, w_ref, o_ref):
    o_ref[...] = jnp.dot(x_ref[...], w_ref[...], preferred_element_type=F32)


def _proj(xb, w_all, layer):
    mt, d = xb.shape
    n = w_all.shape[-1]
    tm = _pick_tile(mt, (768, 512, 256, 128, 8))
    tn = _pick_tile(n, (1152, 640, 128))
    return pl.pallas_call(
        _proj_kernel, grid=(n // tn, mt // tm),
        in_specs=[pl.BlockSpec((tm, d), lambda j, i: (i, 0)),
                  pl.BlockSpec((None, d, tn), lambda j, i: (layer, 0, j))],
        out_specs=pl.BlockSpec((tm, tn), lambda j, i: (i, j)),
        out_shape=jax.ShapeDtypeStruct((mt, n), F32),
        compiler_params=_cparams(("arbitrary", "arbitrary"), 48), name="in_proj")(xb, w_all)


def _outproj_kernel(hn_ref, hd_ref, hs_ref, wn_ref, wd_ref, ws_ref, x_ref, g_ref, b_ref, rw_ref, rb_ref,
                    xf_ref, lg_ref, *, alpha):
    h = jnp.dot(hn_ref[...], wn_ref[...], preferred_element_type=F32)
    h = h + jnp.dot(hd_ref[...], wd_ref[...], preferred_element_type=F32)
    h = h + jnp.dot(hs_ref[...], ws_ref[...], preferred_element_type=F32)
    y = _ln_rows(alpha * x_ref[...] + h, g_ref[...], b_ref[...])
    xf_ref[...] = y
    yh = y.astype(BF16)
    yl = (y - yh.astype(F32)).astype(BF16)
    rw = rw_ref[...]
    wh = rw.astype(BF16)
    wl = (rw - wh.astype(F32)).astype(BF16)
    lg = jnp.dot(yh, wh, preferred_element_type=F32)
    lg = lg + jnp.dot(yh, wl, preferred_element_type=F32)
    lg = lg + jnp.dot(yl, wh, preferred_element_type=F32)
    lg_ref[...] = lg + rb_ref[...]


def _outproj(hn, hd, hs, w_out_b, x, g, b, rw_pad, rb_pad, layer, alpha):
    mt, d = x.shape
    tm = _pick_tile(mt, (256, 128, 8))
    wn, wdw, wsw = hn.shape[1], hd.shape[1], hs.shape[1]
    assert wn == wdw and (wn + wdw) % wsw == 0
    row = lambda w: pl.BlockSpec((tm, w), lambda i: (i, 0))
    vec = pl.BlockSpec((None, 1, d), lambda i: (layer, 0, 0))
    return pl.pallas_call(
        functools.partial(_outproj_kernel, alpha=alpha), grid=(mt // tm,),
        in_specs=[row(wn), row(wdw), row(wsw),
                  pl.BlockSpec((None, wn, d), lambda i: (layer, 0, 0)),
                  pl.BlockSpec((None, wdw, d), lambda i: (layer, 1, 0)),
                  pl.BlockSpec((None, wsw, d), lambda i: (layer, (wn + wdw) // wsw, 0)),
                  row(d), vec, vec,
                  pl.BlockSpec((None, d, LANE), lambda i: (layer, 0, 0)),
                  pl.BlockSpec((None, 1, LANE), lambda i: (layer, 0, 0))],
        out_specs=[row(d), row(LANE)],
        out_shape=[jax.ShapeDtypeStruct((mt, d), F32), jax.ShapeDtypeStruct((mt, LANE), F32)],
        compiler_params=_cparams(("arbitrary",), 48), name="out_proj_ln")(
            hn, hd, hs, w_out_b, w_out_b, w_out_b, x, g, b, rw_pad, rb_pad)


def _route_kernel(lg_ref, e_ref, r_ref, g_ref, cnt_ref, carry, *, tb, n_valid):
    i = pl.program_id(0)

    @pl.when(i == 0)
    def _():
        carry[...] = jnp.zeros_like(carry)

    work = lg_ref[...]
    lane = lax.broadcasted_iota(I32, (tb, LANE), 1)
    lane_f = lane.astype(F32)
    row = i * tb + lax.broadcasted_iota(I32, (tb, 1), 0)
    valid = row < n_valid
    vals, idxs = [], []
    for _ in range(TOP_K):
        m = jnp.max(work, axis=-1, keepdims=True)
        idx = jnp.min(jnp.where(work == m, lane_f, float(LANE)), axis=-1, keepdims=True)
        vals.append(m)
        idxs.append(idx)
        work = jnp.where(lane_f == idx, -jnp.inf, work)
    ex = [jnp.exp(v - vals[0]) for v in vals]
    den = ex[0] + ex[1] + ex[2] + ex[3]
    hot = [(lane_f == idx) & valid for idx in idxs]
    a = jnp.zeros((tb, LANE), F32)
    for hk in hot:
        a = a + jnp.where(hk, 1.0, 0.0)
    tri = (lax.broadcasted_iota(I32, (tb, tb), 0) > lax.broadcasted_iota(I32, (tb, tb), 1))
    before = jnp.dot(jnp.where(tri, 1.0, 0.0).astype(BF16), a.astype(BF16), preferred_element_type=F32) + carry[...]
    e_out = jnp.zeros((tb, LANE), F32)
    r_out = jnp.zeros((tb, LANE), F32)
    g_out = jnp.zeros((tb, LANE), F32)
    for k in range(TOP_K):
        rank = jnp.sum(jnp.where(hot[k], before, 0.0), axis=-1, keepdims=True)
        e_out = jnp.where(lane == k, idxs[k], e_out)
        r_out = jnp.where(lane == k, rank, r_out)
        g_out = jnp.where(lane == k, jnp.where(valid, ex[k] / den, 0.0), g_out)
    e_ref[...] = e_out.astype(I32)
    r_ref[...] = r_out.astype(I32)
    g_ref[...] = g_out
    carry[...] = carry[...] + jnp.sum(a, axis=0, keepdims=True)
    cnt_ref[...] = carry[...]


def _route(logits, n_valid):
    mt = logits.shape[0]
    tb = _pick_tile(mt, (256, 128, 8))
    row = pl.BlockSpec((tb, LANE), lambda i: (i, 0))
    return pl.pallas_call(
        functools.partial(_route_kernel, tb=tb, n_valid=n_valid), grid=(mt // tb,),
        in_specs=[row], out_specs=[row, row, row, pl.BlockSpec((1, LANE), lambda i: (0, 0))],
        out_shape=[jax.ShapeDtypeStruct((mt, LANE), I32), jax.ShapeDtypeStruct((mt, LANE), I32),
                   jax.ShapeDtypeStruct((mt, LANE), F32), jax.ShapeDtypeStruct((1, LANE), F32)],
        scratch_shapes=[pltpu.VMEM((1, LANE), F32)],
        compiler_params=_cparams(("arbitrary",)), name="moe_route")(logits)


def _scatter_kernel(dest_ref, x_ref, xs_in, xs_out, sem, *, ts):
    del xs_in
    i = pl.program_id(0)

    def issue(t, c):
        for k in range(TOP_K):
            d = dest_ref[(i * ts + t) * TOP_K + k]
            pltpu.make_async_copy(x_ref.at[pl.ds(t, 1)], xs_out.at[pl.ds(d, 1)], sem).start()
        return c

    lax.fori_loop(0, ts, issue, 0)

    def drain(t, c):
        for k in range(TOP_K):
            pltpu.make_async_copy(x_ref.at[pl.ds(0, 1)], xs_out.at[pl.ds(0, 1)], sem).wait()
        return c

    lax.fori_loop(0, ts, drain, 0)


def _scatter_rows(dest_flat, x, xs_zero):
    mt, d = x.shape
    ts = _pick_tile(mt, (256, 128, 8))
    return pl.pallas_call(
        functools.partial(_scatter_kernel, ts=ts),
        grid_spec=pltpu.PrefetchScalarGridSpec(
            num_scalar_prefetch=1, grid=(mt // ts,),
            in_specs=[pl.BlockSpec((ts, d), lambda i, dest: (i, 0)), pl.BlockSpec(memory_space=pl.ANY)],
            out_specs=pl.BlockSpec(memory_space=pl.ANY),
            scratch_shapes=[pltpu.SemaphoreType.DMA]),
        out_shape=jax.ShapeDtypeStruct(xs_zero.shape, xs_zero.dtype),
        input_output_aliases={2: 0},
        compiler_params=_cparams(("arbitrary",)), name="moe_scatter")(dest_flat, x, xs_zero)


def _weights_changed(be_ref):
    b = pl.program_id(1)
    return (b == 0) | (be_ref[b] != be_ref[jnp.maximum(b - 1, 0)])


def _moe_up_kernel(be_ref, na_ref, x_ref, w_ref, b_ref, p_ref, o_ref, wb_s):
    active = pl.program_id(1) < na_ref[0]

    @pl.when(active & _weights_changed(be_ref))
    def _():
        wb_s[...] = w_ref[...].astype(BF16)

    @pl.when(active)
    def _():
        h = jnp.dot(x_ref[...].astype(BF16), wb_s[...], preferred_element_type=F32) + b_ref[...]
        glu = jnp.minimum(h, SWIGLU_LIMIT)
        gated = glu * jax.nn.sigmoid(SWIGLU_ALPHA * glu)
        lin1 = jnp.clip(h, -SWIGLU_LIMIT, SWIGLU_LIMIT) + 1.0
        for k in range(o_ref.shape[1] // LANE):
            pieces = []
            for c0 in (2 * k * LANE, (2 * k + 1) * LANE):
                nxt = pltpu.roll(lin1[:, c0:c0 + LANE], LANE - 1, 1)
                pieces.append((gated[:, c0:c0 + LANE] * nxt).astype(BF16))
            prod = jnp.concatenate(pieces, axis=1)
            o_ref[:, k * LANE:(k + 1) * LANE] = jnp.dot(prod, p_ref[...], preferred_element_type=F32).astype(BF16)

    @pl.when(jnp.logical_not(active))
    def _():
        o_ref[...] = jnp.zeros_like(o_ref)


def _moe_up(block_e, n_act, xs, w_up, b_up4, p_even, layer, n_blocks, tme):
    d = xs.shape[1]
    f2 = w_up.shape[-1]
    tn2 = _pick_tile(f2, (1024, 512, 256))
    rows = lambda j, b, be, na: (jnp.minimum(b, na[0] - 1), 0)
    return pl.pallas_call(
        _moe_up_kernel,
        grid_spec=pltpu.PrefetchScalarGridSpec(
            num_scalar_prefetch=2, grid=(f2 // tn2, n_blocks),
            in_specs=[pl.BlockSpec((tme, d), rows),
                      pl.BlockSpec((None, None, d, tn2), lambda j, b, be, na: (layer, be[b], 0, j)),
                      pl.BlockSpec((None, None, 1, tn2), lambda j, b, be, na: (layer, be[b], 0, j)),
                      pl.BlockSpec((2 * LANE, LANE), lambda j, b, be, na: (0, 0))],
            out_specs=pl.BlockSpec((tme, tn2 // 2), lambda j, b, be, na: (b, j)),
            scratch_shapes=[pltpu.VMEM((d, tn2), BF16)]),
        out_shape=jax.ShapeDtypeStruct((n_blocks * tme, f2 // 2), BF16),
        compiler_params=_cparams(("arbitrary", "arbitrary"), 48), name="moe_up")(
            block_e, n_act, xs, w_up, b_up4, p_even)


def _moe_down_kernel(be_ref, na_ref, a_ref, w_ref, b_ref, o_ref, wb_s):
    active = pl.program_id(1) < na_ref[0]

    @pl.when(active & _weights_changed(be_ref))
    def _():
        wb_s[...] = w_ref[...].astype(BF16)

    @pl.when(active)
    def _():
        o_ref[...] = jnp.dot(a_ref[...], wb_s[...], preferred_element_type=F32) + b_ref[...]

    @pl.when(jnp.logical_not(active))
    def _():
        o_ref[...] = jnp.zeros_like(o_ref)


def _moe_down(block_e, n_act, act, w_down, b_down4, layer, n_blocks, tme):
    f = act.shape[1]
    d = w_down.shape[-1]
    tn = _pick_tile(d, (1024, 512, 256, 128))
    rows = lambda j, b, be, na: (jnp.minimum(b, na[0] - 1), 0)
    return pl.pallas_call(
        _moe_down_kernel,
        grid_spec=pltpu.PrefetchScalarGridSpec(
            num_scalar_prefetch=2, grid=(d // tn, n_blocks),
            in_specs=[pl.BlockSpec((tme, f), rows),
                      pl.BlockSpec((None, None, f, tn), lambda j, b, be, na: (layer, be[b], 0, j)),
                      pl.BlockSpec((None, None, 1, tn), lambda j, b, be, na: (layer, be[b], 0, j))],
            out_specs=pl.BlockSpec((tme, tn), lambda j, b, be, na: (b, j)),
            scratch_shapes=[pltpu.VMEM((f, tn), BF16)]),
        out_shape=jax.ShapeDtypeStruct((n_blocks * tme, d), F32),
        compiler_params=_cparams(("arbitrary", "arbitrary"), 48), name="moe_down")(
            block_e, n_act, act, w_down, b_down4)


def _combine_kernel(dest_ref, y_hbm, x_ref, gate_ref, g_ref, b_ref, xf_ref, xb_ref, buf, sem, *, tc, alpha):
    i = pl.program_id(0)

    def issue(t, c):
        for k in range(TOP_K):
            d = dest_ref[(i * tc + t) * TOP_K + k]
            pltpu.make_async_copy(y_hbm.at[pl.ds(d, 1)], buf.at[k, pl.ds(t, 1)], sem).start()
        return c

    lax.fori_loop(0, tc, issue, 0)

    def drain(t, c):
        for k in range(TOP_K):
            pltpu.make_async_copy(y_hbm.at[pl.ds(0, 1)], buf.at[k, pl.ds(0, 1)], sem).wait()
        return c

    lax.fori_loop(0, tc, drain, 0)
    gate = gate_ref[...]
    y = gate[:, 0:1] * buf[0]
    for k in range(1, TOP_K):
        y = y + gate[:, k:k + 1] * buf[k]
    out = _ln_rows(alpha * x_ref[...] + y, g_ref[...], b_ref[...])
    xf_ref[...] = out
    xb_ref[...] = out.astype(BF16)


def _combine(dest_flat, y_rows, x1, gates, g, b, layer, alpha):
    mt, d = x1.shape
    tc = _pick_tile(mt, (128, 8))
    row = lambda w: pl.BlockSpec((tc, w), lambda i, dest: (i, 0))
    vec = pl.BlockSpec((None, 1, d), lambda i, dest: (layer, 0, 0))
    return pl.pallas_call(
        functools.partial(_combine_kernel, tc=tc, alpha=alpha),
        grid_spec=pltpu.PrefetchScalarGridSpec(
            num_scalar_prefetch=1, grid=(mt // tc,),
            in_specs=[pl.BlockSpec(memory_space=pl.ANY), row(d), row(LANE), vec, vec],
            out_specs=[row(d), row(d)],
            scratch_shapes=[pltpu.VMEM((TOP_K, tc, d), F32), pltpu.SemaphoreType.DMA]),
        out_shape=[jax.ShapeDtypeStruct((mt, d), F32), jax.ShapeDtypeStruct((mt, d), BF16)],
        compiler_params=_cparams(("arbitrary",), 32), name="moe_combine_ln")(dest_flat, y_rows, x1, gates, g, b)


def _moe_layer(x1, logits, n_valid, mw, ln_g, ln_b, layer, alpha):
    mt, d = x1.shape
    tme = 256
    n_blocks = -(-(n_valid * TOP_K) // tme) + N_EXPERTS
    e_i, r_i, gates, counts = _route(logits, n_valid)
    cnt = counts[0, :N_EXPERTS].astype(I32)
    padded = (cnt + tme - 1) // tme * tme
    pend = jnp.cumsum(padded)
    pstart = pend - padded
    n_act = (pend[-1] // tme).astype(I32)
    blk = jnp.arange(n_blocks, dtype=I32)
    block_e = jnp.minimum(jnp.sum((blk[:, None] * tme >= pend[None, :]).astype(I32), axis=1), N_EXPERTS - 1)
    block_e = jnp.where(blk < n_act, block_e, block_e[jnp.maximum(n_act - 1, 0)]).astype(I32)
    e4 = e_i[:, :TOP_K]
    dest = pstart[jnp.clip(e4, 0, N_EXPERTS - 1)] + r_i[:, :TOP_K]
    tok = jnp.arange(mt, dtype=I32)[:, None]
    valid = tok < n_valid
    trash = n_blocks * tme + (tok - n_valid) * TOP_K + jnp.arange(TOP_K, dtype=I32)[None, :]
    dest_scatter = jnp.where(valid, dest, trash).astype(I32).reshape(-1)
    dest_gather = jnp.where(valid, dest, 0).astype(I32).reshape(-1)
    r_tot = n_blocks * tme + (mt - n_valid) * TOP_K
    r_tot = -(-r_tot // 8) * 8
    xs = _scatter_rows(dest_scatter, x1, jnp.zeros((r_tot, d), F32))
    n_act1 = n_act.reshape(1)
    act = _moe_up(block_e, n_act1, xs, mw["w_up"], mw["b_up"], mw["p_even"], layer, n_blocks, tme)
    y_rows = _moe_down(block_e, n_act1, act, mw["w_down"], mw["b_down"], layer, n_blocks, tme)
    return _combine(dest_gather, y_rows, x1, gates, ln_g, ln_b, layer, alpha)


def _flash_step(qb, kj, vj, bias_list, shift_list, m, l, acc):
    tq = bias_list[0].shape[0]
    s = lax.dot_general(qb, kj, NT_DIMS, preferred_element_type=F32)
    t = jnp.concatenate([s[r * tq:(r + 1) * tq] + b for r, b in enumerate(bias_list)], axis=0)
    shift = jnp.concatenate([jnp.full((tq, 1), 1.0, F32) * sh for sh in shift_list], axis=0)
    m_new = jnp.maximum(m, jnp.max(t, axis=-1, keepdims=True) + shift)
    a = jnp.exp(m - m_new)
    p = jnp.exp(t - (m_new - shift))
    l = a * l + jnp.sum(p, axis=-1, keepdims=True)
    acc = a * acc + jnp.dot(p.astype(BF16), vj, preferred_element_type=F32)
    return m_new, l, acc


def _nsa_kernel(q_ref, ck_ref, cv_ref, sk_ref, sv_ref, wk_ref, wv_ref, gt_ref, wpk_ref, wpv_ref, e_ref,
                o_ref, kcmp_s, vcmp_s, skb, svb, wkb, wvb, km_s, *, t_len, tq, tk):
    g = pl.program_id(1)
    i = pl.program_id(2)
    nb = t_len // NSA_BLOCK
    n_sel = min(N_SELECT, -(-t_len // NSA_BLOCK))

    @pl.when(i == 0)
    def _():
        ck = ck_ref[...].reshape(nb, NSA_BLOCK, HEAD_DIM)
        kcmp_s[...] = jnp.sum(ck * wpk_ref[...][None, :, :], axis=1).astype(BF16)
        cv = cv_ref[...].reshape(nb, NSA_BLOCK, HEAD_DIM)
        vcmp_s[...] = jnp.sum(cv * wpv_ref[...][None, :, :], axis=1).astype(BF16)
        skb[...] = sk_ref[...].astype(BF16)
        svb[...] = sv_ref[...].astype(BF16)
        wkb[...] = wk_ref[...].astype(BF16)
        wvb[...] = wv_ref[...].astype(BF16)

    s0 = i * tq
    slopes = [jnp.where(g == 0, float(SL_A[0][r]), float(SL_A[1][r])) for r in range(A_REP)]
    q = q_ref[...]
    q3b = (jnp.concatenate([q[:, r * HEAD_DIM:(r + 1) * HEAD_DIM] for r in range(A_REP)], axis=0)
           * (HEAD_DIM ** -0.5)).astype(BF16)
    qpos = s0 + lax.broadcasted_iota(I32, (tq, 1), 0)

    sc = lax.dot_general(q3b, kcmp_s[...], NT_DIMS, preferred_element_type=F32)
    blk = lax.broadcasted_iota(I32, (tq, nb), 1)
    endj = blk * NSA_BLOCK + (NSA_BLOCK - 1)
    okc = endj <= qpos
    distc = (qpos - endj).astype(F32)
    imp = jnp.zeros((tq, nb), F32)
    o_cmp = []
    for r in range(A_REP):
        sm = jnp.where(okc, sc[r * tq:(r + 1) * tq] - slopes[r] * distc, NEG)
        m = jnp.max(sm, axis=-1, keepdims=True)
        e = jnp.where(okc, jnp.exp(sm - m), 0.0)
        p = e / jnp.maximum(jnp.sum(e, axis=-1, keepdims=True), 1e-30)
        imp = imp + p
        o_cmp.append(jnp.dot(p.astype(BF16), vcmp_s[...], preferred_element_type=F32))

    cur = lax.shift_right_logical(qpos, int(math.log2(NSA_BLOCK)))
    score = jnp.where(blk == cur, jnp.inf, jnp.where(blk > cur, -jnp.inf, imp))
    cnt = jnp.zeros((tq, nb), F32)
    for c in range(nb):
        col = score[:, c:c + 1]
        beats = (col > score) | ((col == score) & (blk > c))
        cnt = cnt + jnp.where(beats, 1.0, 0.0)
    sel = jnp.where(cnt < n_sel, 1.0, 0.0).astype(BF16)
    km = jnp.dot(sel, e_ref[...], preferred_element_type=F32)
    for c in range(t_len // tk):
        km_s[c] = km[:, c * tk:(c + 1) * tk]

    rc = (lax.broadcasted_iota(I32, (tq, tk), 0) - lax.broadcasted_iota(I32, (tq, tk), 1)).astype(F32)
    bias0 = [-slopes[r] * rc for r in range(A_REP)]
    m0 = jnp.full((A_REP * tq, 1), NEG, F32)
    l0 = jnp.zeros((A_REP * tq, 1), F32)
    a0 = jnp.zeros((A_REP * tq, HEAD_DIM), F32)
    j_diag = s0 // tk

    def slc_step(j, carry, causal):
        off = pl.multiple_of(j * tk, tk)
        delta = (s0 - j * tk).astype(F32)
        ok = km_s[j] > 0.5
        if causal:
            ok = ok & (rc + delta >= 0.0)
        bias = [jnp.where(ok, bias0[r], NEG) for r in range(A_REP)]
        shift = [-slopes[r] * delta for r in range(A_REP)]
        return _flash_step(q3b, skb[pl.ds(off, tk), :], svb[pl.ds(off, tk), :], bias, shift, *carry)

    carry = lax.fori_loop(0, j_diag, lambda j, c: slc_step(j, c, False), (m0, l0, a0))
    _, l_s, acc_s = slc_step(j_diag, carry, True)
    o_slc = acc_s / l_s

    def win_body(j, carry):
        off = pl.multiple_of(j * tk, tk)
        delta = (s0 - j * tk).astype(F32)
        dist = rc + delta
        ok = (dist >= 0.0) & (dist < float(WINDOW))
        bias = [jnp.where(ok, bias0[r], NEG) for r in range(A_REP)]
        shift = [-slopes[r] * delta for r in range(A_REP)]
        return _flash_step(q3b, wkb[pl.ds(off, tk), :], wvb[pl.ds(off, tk), :], bias, shift, *carry)

    j_lo = jnp.maximum(s0 - (WINDOW - 1), 0) // tk
    _, l_w, acc_w = lax.fori_loop(j_lo, j_diag + 1, win_body, (m0, l0, a0))
    o_win = acc_w / l_w

    gt = jax.nn.sigmoid(gt_ref[...])
    for r in range(A_REP):
        c = []
        for n in range(N_GATES):
            i0 = (0 * A_REP + r) * N_GATES + n
            i1 = (1 * A_REP + r) * N_GATES + n
            c.append(jnp.where(g == 0, gt[:, i0:i0 + 1], gt[:, i1:i1 + 1]))
        sl = slice(r * tq, (r + 1) * tq)
        out = c[0] * o_cmp[r] + c[1] * o_slc[sl] + c[2] * o_win[sl]
        o_ref[:, r * HEAD_DIM:(r + 1) * HEAD_DIM] = out.astype(BF16)


def _nsa_prompt(proj, cmp_wk, cmp_wv, e_mat, layer, n_b, t_len):
    tq = 128
    tk = _pick_tile(t_len, PROMPT_TK)
    nq = t_len // tq
    nb = t_len // NSA_BLOCK
    gw = A_REP * HEAD_DIM
    seq = lambda cb: pl.BlockSpec((t_len, HEAD_DIM), lambda b, g, i, cb=cb: (b, cb + g))
    wp = pl.BlockSpec((None, NSA_BLOCK, HEAD_DIM), lambda b, g, i: (layer, 0, 0))
    return pl.pallas_call(
        functools.partial(_nsa_kernel, t_len=t_len, tq=tq, tk=tk), grid=(n_b, A_KV, nq),
        in_specs=[pl.BlockSpec((tq, gw), lambda b, g, i: (b * nq + i, g)),
                  seq(CB_NSA), seq(CB_NSA + 2), seq(CB_NSA + 4), seq(CB_NSA + 6), seq(CB_WIN), seq(CB_WIN + 2),
                  pl.BlockSpec((tq, LANE), lambda b, g, i: (b * nq + i, CB_GATE)),
                  wp, wp, pl.BlockSpec((nb, t_len), lambda b, g, i: (0, 0))],
        out_specs=pl.BlockSpec((tq, gw), lambda b, g, i: (b * nq + i, g)),
        out_shape=jax.ShapeDtypeStruct((n_b * t_len, A_HEADS * HEAD_DIM), BF16),
        scratch_shapes=[pltpu.VMEM((nb, HEAD_DIM), BF16), pltpu.VMEM((nb, HEAD_DIM), BF16),
                        pltpu.VMEM((t_len, HEAD_DIM), BF16), pltpu.VMEM((t_len, HEAD_DIM), BF16),
                        pltpu.VMEM((t_len, HEAD_DIM), BF16), pltpu.VMEM((t_len, HEAD_DIM), BF16),
                        pltpu.VMEM((t_len // tk, tq, tk), F32)],
        compiler_params=_cparams(("arbitrary", "arbitrary", "arbitrary"), 56), name="nsa_prompt")(
            proj, proj, proj, proj, proj, proj, proj, proj, cmp_wk, cmp_wv, e_mat)


def _diff_kernel(lam_ref, q_ref, k_ref, v_ref, g_ref, o_ref, kb, vb, *, tq, tk, lam_init):
    h = pl.program_id(1)
    i = pl.program_id(2)

    @pl.when(i == 0)
    def _():
        kb[...] = k_ref[...].astype(BF16)
        vb[...] = v_ref[...].astype(BF16)

    s0 = i * tq
    slope = _select_by_index(h, [float(s) for s in SL_B])
    q = q_ref[...] * (B_HALF ** -0.5)
    lane = lax.broadcasted_iota(I32, (tq, HEAD_DIM), 1)
    qs = jnp.concatenate([jnp.where(lane < B_HALF, q, 0.0), jnp.where(lane >= B_HALF, q, 0.0)], axis=0).astype(BF16)
    rc = (lax.broadcasted_iota(I32, (tq, tk), 0) - lax.broadcasted_iota(I32, (tq, tk), 1)).astype(F32)

    bias0 = -slope * rc
    j_diag = s0 // tk

    def step(j, carry, causal):
        off = pl.multiple_of(j * tk, tk)
        delta = (s0 - j * tk).astype(F32)
        bias = jnp.where(rc + delta >= 0.0, bias0, NEG) if causal else bias0
        shift = -slope * delta
        return _flash_step(qs, kb[pl.ds(off, tk), :], vb[pl.ds(off, tk), :], [bias, bias], [shift, shift], *carry)

    m0 = jnp.full((2 * tq, 1), NEG, F32)
    l0 = jnp.zeros((2 * tq, 1), F32)
    a0 = jnp.zeros((2 * tq, HEAD_DIM), F32)
    carry = lax.fori_loop(0, j_diag, lambda j, c: step(j, c, False), (m0, l0, a0))
    _, l, acc = step(j_diag, carry, True)
    o = acc / l
    o = o[:tq] - lam_ref[0] * o[tq:]
    o = o * lax.rsqrt(jnp.mean(o * o, axis=-1, keepdims=True) + RMS_EPS) * g_ref[...] * (1.0 - lam_init)
    o_ref[...] = o.astype(BF16)


def _diff_prompt(lam, proj, diff_g, layer, n_b, t_len, lam_init):
    tq = 128
    tk = _pick_tile(t_len, PROMPT_TK)
    nq = t_len // tq
    seq = lambda cb: pl.BlockSpec((t_len, HEAD_DIM), lambda b, h, i, cb=cb: (b, cb + h))
    return pl.pallas_call(
        functools.partial(_diff_kernel, tq=tq, tk=tk, lam_init=lam_init), grid=(n_b, B_HEADS, nq),
        in_specs=[pl.BlockSpec(memory_space=pltpu.SMEM),
                  pl.BlockSpec((tq, HEAD_DIM), lambda b, h, i: (b * nq + i, CB_BQ + h)),
                  seq(CB_BK), seq(CB_BV),
                  pl.BlockSpec((None, 1, HEAD_DIM), lambda b, h, i: (layer, 0, 0))],
        out_specs=pl.BlockSpec((tq, HEAD_DIM), lambda b, h, i: (b * nq + i, h)),
        out_shape=jax.ShapeDtypeStruct((n_b * t_len, B_HEADS * HEAD_DIM), BF16),
        scratch_shapes=[pltpu.VMEM((t_len, HEAD_DIM), BF16), pltpu.VMEM((t_len, HEAD_DIM), BF16)],
        compiler_params=_cparams(("arbitrary", "arbitrary", "arbitrary"), 32), name="diff_prompt")(
            lam, proj, proj, proj, diff_g)


def _sgu_kernel(u_ref, v_ref, g_ref, b_ref, w_ref, bt_ref, o_ref, *, n_chunks):
    u = jax.nn.gelu(u_ref[...])
    v = _ln_rows(jax.nn.gelu(v_ref[...]), g_ref[...], b_ref[...])
    tri = lax.broadcasted_iota(I32, (CHUNK, CHUNK), 0) >= lax.broadcasted_iota(I32, (CHUNK, CHUNK), 1)
    for gi in range(C_GROUPS):
        w = jnp.where(tri, w_ref[gi], 0.0).astype(BF16)
        cols = slice(gi * HEAD_DIM, (gi + 1) * HEAD_DIM)
        for c in range(n_chunks):
            rows = slice(c * CHUNK, (c + 1) * CHUNK)
            s = jnp.dot(w, v[rows, cols].astype(BF16), preferred_element_type=F32) + bt_ref[:, gi:gi + 1]
            o_ref[rows, cols] = (u[rows, cols] * s).astype(BF16)


def _sgu_prompt(proj, ln_g, ln_b, sgu_w, sgu_bt, layer, m_p):
    n_chunks = _pick_tile(m_p // CHUNK, (4, 2, 1))
    tr = n_chunks * CHUNK
    vec = pl.BlockSpec((None, 1, C_WIDTH), lambda i: (layer, 0, 0))
    return pl.pallas_call(
        functools.partial(_sgu_kernel, n_chunks=n_chunks), grid=(m_p // tr,),
        in_specs=[pl.BlockSpec((tr, C_WIDTH), lambda i: (i, CB_CU // C_GROUPS)),
                  pl.BlockSpec((tr, C_WIDTH), lambda i: (i, CB_CV // C_GROUPS)),
                  vec, vec,
                  pl.BlockSpec((None, C_GROUPS, CHUNK, CHUNK), lambda i: (layer, 0, 0, 0)),
                  pl.BlockSpec((None, CHUNK, C_GROUPS), lambda i: (layer, 0, 0))],
        out_specs=pl.BlockSpec((tr, C_WIDTH), lambda i: (i, 0)),
        out_shape=jax.ShapeDtypeStruct((m_p, C_WIDTH), BF16),
        compiler_params=_cparams(("arbitrary",)), name="sgu_prompt")(proj, proj, ln_g, ln_b, sgu_w, sgu_bt)


NSA_SLABS = 4 * A_KV
DIFF_SLABS = 2 * B_HEADS
WIN_SLABS = 2 * A_KV


def _page_specs(n_pages_step, rows, layer):
    return [pl.BlockSpec((None, None, rows, HEAD_DIM),
                         lambda b, c, pt, i=i: (layer, pt[b, c * n_pages_step + i], 0, 0))
            for i in range(n_pages_step)]


def _slab(ref, slab, n_slabs):
    return ref[pl.ds(slab, ref.shape[0] // n_slabs, stride=n_slabs), :]


def _pool_kernel(pt_ref, *refs, pg):
    del pt_ref
    pages, wk_ref, wv_ref, o_ref = refs[:pg], refs[pg], refs[pg + 1], refs[pg + 2]
    per_page = pages[0].shape[0] // NSA_SLABS // NSA_BLOCK
    for i in range(pg):
        for slab in range(2 * A_KV):
            x = _slab(pages[i], slab, NSA_SLABS)
            w = wk_ref[...] if slab < A_KV else wv_ref[...]
            for hb in range(per_page):
                r = i * per_page + hb
                o_ref[r:r + 1, slab * HEAD_DIM:(slab + 1) * HEAD_DIM] = jnp.sum(
                    x[hb * NSA_BLOCK:(hb + 1) * NSA_BLOCK] * w, axis=0, keepdims=True)


def _sample_pool(page_table, cache_nsa4, cmp_wk, cmp_wv, layer, pg):
    n_s, n_pages = page_table.shape
    rows = cache_nsa4.shape[2]
    half = 2 * A_KV * HEAD_DIM
    per_page = rows // NSA_SLABS // NSA_BLOCK
    nc = n_pages // pg
    wp = pl.BlockSpec((None, NSA_BLOCK, HEAD_DIM), lambda b, c, pt: (layer, 0, 0))
    return pl.pallas_call(
        functools.partial(_pool_kernel, pg=pg),
        grid_spec=pltpu.PrefetchScalarGridSpec(
            num_scalar_prefetch=1, grid=(n_s, nc),
            in_specs=_page_specs(pg, rows, layer) + [wp, wp],
            out_specs=pl.BlockSpec((None, pg * per_page, half), lambda b, c, pt: (b, c, 0))),
        out_shape=jax.ShapeDtypeStruct((n_s, n_pages * per_page, half), F32),
        compiler_params=_cparams(("arbitrary", "arbitrary"), 32), name="sample_cmp_pool")(
            page_table, *([cache_nsa4] * pg), cmp_wk, cmp_wv)


def _q3_rows(prow, g):
    rowi = lax.broadcasted_iota(I32, (8, HEAD_DIM), 0)
    q3 = jnp.zeros((8, HEAD_DIM), F32)
    for r in range(A_REP):
        c0 = (CB_AQ + g * A_REP + r) * HEAD_DIM
        q3 = jnp.where(rowi == r, jnp.broadcast_to(prow[:, c0:c0 + HEAD_DIM], (8, HEAD_DIM)), q3)
    return q3 * (HEAD_DIM ** -0.5)


def _select_kernel(p_ref, kv_ref, sl_ref, e_ref, ocmp_ref, km_ref, *, p_len, nbp, bpc):
    prow = p_ref[...]
    n_blocks = nbp + 1
    n_sel = min(N_SELECT, n_blocks)
    blk = lax.broadcasted_iota(I32, (8, nbp), 1)
    dist = (p_len - (blk * NSA_BLOCK + NSA_BLOCK - 1)).astype(F32)
    lane_f = lax.broadcasted_iota(I32, (1, nbp), 1).astype(F32)
    rowi = lax.broadcasted_iota(I32, (8, nbp), 0)
    sel_rows = jnp.zeros((8, nbp), F32)
    outs = []
    for g in range(A_KV):
        q3 = _q3_rows(prow, g).astype(BF16)
        kc = kv_ref[:, g * HEAD_DIM:(g + 1) * HEAD_DIM].astype(BF16)
        vc = kv_ref[:, (A_KV + g) * HEAD_DIM:(A_KV + g + 1) * HEAD_DIM].astype(BF16)
        s = lax.dot_general(q3, kc, NT_DIMS, preferred_element_type=F32) - sl_ref[g] * dist
        m = jnp.max(s, axis=-1, keepdims=True)
        e = jnp.exp(s - m)
        p = e / jnp.maximum(jnp.sum(e, axis=-1, keepdims=True), 1e-30)
        o = jnp.dot(p.astype(BF16), vc, preferred_element_type=F32)
        outs += [o[r:r + 1] for r in range(A_REP)]
        work = p[0:1] + p[1:2] + p[2:3]
        sel = jnp.zeros((1, nbp), F32)
        for _ in range(n_sel - 1):
            mx = jnp.max(work, axis=-1, keepdims=True)
            idx = jnp.min(jnp.where(work == mx, lane_f, float(nbp)), axis=-1, keepdims=True)
            hit = lane_f == idx
            sel = jnp.where(hit, 1.0, sel)
            work = jnp.where(hit, -jnp.inf, work)
        sel_rows = jnp.where(rowi == g, sel, sel_rows)
    ocmp_ref[...] = jnp.concatenate(outs, axis=1)
    kpc = bpc * NSA_BLOCK
    for c in range(nbp // bpc):
        in_chunk = (blk >= c * bpc) & (blk < (c + 1) * bpc)
        km_ref[:, c * kpc:(c + 1) * kpc] = jnp.dot(jnp.where(in_chunk, sel_rows, 0.0).astype(BF16), e_ref[...],
                                                   preferred_element_type=F32)


def _sample_select(prow, kvcmp, sl_cols, e_small, p_len, bpc):
    n_s = prow.shape[0]
    nbp = kvcmp.shape[1]
    half = kvcmp.shape[2]
    return pl.pallas_call(
        functools.partial(_select_kernel, p_len=p_len, nbp=nbp, bpc=bpc), grid=(n_s,),
        in_specs=[pl.BlockSpec((None, 1, N_PROJ), lambda b: (b, 0, 0)),
                  pl.BlockSpec((None, nbp, half), lambda b: (b, 0, 0)),
                  pl.BlockSpec((A_KV, 8, 1), lambda b: (0, 0, 0)),
                  pl.BlockSpec((nbp, bpc * NSA_BLOCK), lambda b: (0, 0))],
        out_specs=[pl.BlockSpec((None, 1, A_HEADS * HEAD_DIM), lambda b: (b, 0, 0)),
                   pl.BlockSpec((None, 8, p_len), lambda b: (b, 0, 0))],
        out_shape=[jax.ShapeDtypeStruct((n_s, 1, A_HEADS * HEAD_DIM), F32),
                   jax.ShapeDtypeStruct((n_s, 8, p_len), F32)],
        compiler_params=_cparams(("arbitrary",), 32), name="sample_select")(prow, kvcmp, sl_cols, e_small)


def _decode_update(s, ok, vb_list, m_ref, l_ref, acc_ref, page):
    sm = s if ok is None else jnp.where(ok, s, NEG)
    m_old = m_ref[...]
    m_new = jnp.maximum(m_old, jnp.max(sm, axis=-1, keepdims=True))
    a = jnp.exp(m_old - m_new)
    p = jnp.exp(sm - m_new)
    if ok is not None:
        p = jnp.where(ok, p, 0.0)
    l_ref[...] = a * l_ref[...] + jnp.sum(p, axis=-1, keepdims=True)
    pb = p.astype(BF16)
    pv = jnp.dot(pb[:, 0:page], vb_list[0], preferred_element_type=F32)
    for i in range(1, len(vb_list)):
        pv = pv + jnp.dot(pb[:, i * page:(i + 1) * page], vb_list[i], preferred_element_type=F32)
    acc_ref[...] = a * acc_ref[...] + pv
    m_ref[...] = m_new


def _nsa_decode_kernel(pt_ref, *refs, pg, p_len, n_win):
    del pt_ref
    pages = refs[:pg]
    (p_ref, km_ref, win_ref, ocmp_ref, sl_ref, sg_ref, sb_ref, w00_ref, b0_ref,
     o_ref, osgu_ref, vrow_ref, m_s, l_s, acc_s) = refs[pg:]
    c = pl.program_id(1)
    page = pages[0].shape[0] // NSA_SLABS
    gw = A_KV * HEAD_DIM

    @pl.when(c == 0)
    def _():
        m_s[...] = jnp.full_like(m_s, NEG)
        l_s[...] = jnp.zeros_like(l_s)
        acc_s[...] = jnp.zeros_like(acc_s)

    prow = p_ref[...]
    kpos = c * (pg * page) + lax.broadcasted_iota(I32, (8, pg * page), 1)
    dist = (p_len - kpos).astype(F32)
    for g in range(A_KV):
        q3 = _q3_rows(prow, g).astype(BF16)
        s = jnp.concatenate(
            [lax.dot_general(q3, _slab(pages[i], 2 * A_KV + g, NSA_SLABS).astype(BF16), NT_DIMS,
                             preferred_element_type=F32) for i in range(pg)], axis=1)
        s = s - sl_ref[g] * dist
        ok = jnp.broadcast_to(km_ref[g:g + 1, :], s.shape) > 0.5
        vbs = [_slab(pages[i], 3 * A_KV + g, NSA_SLABS).astype(BF16) for i in range(pg)]
        _decode_update(s, ok, vbs, m_s.at[g], l_s.at[g], acc_s.at[g], page)

    @pl.when(c == pl.num_programs(1) - 1)
    def _():
        gt = jax.nn.sigmoid(prow[:, CB_GATE * LANE:(CB_GATE + 1) * LANE])
        ocmp = ocmp_ref[...]
        wpos = p_len - n_win + lax.broadcasted_iota(I32, (8, n_win), 1)
        wd = (p_len - wpos).astype(F32)
        wok = (wd >= 0.0) & (wd < float(WINDOW))
        outs = []
        for g in range(A_KV):
            q3 = _q3_rows(prow, g)
            q3b = q3.astype(BF16)
            nsa0 = (CB_NSA + 2 * A_KV) * HEAD_DIM
            k_new = prow[:, nsa0 + g * HEAD_DIM:nsa0 + (g + 1) * HEAD_DIM]
            v_new = prow[:, nsa0 + gw + g * HEAD_DIM:nsa0 + gw + (g + 1) * HEAD_DIM]
            s_new = jnp.sum(q3b.astype(F32) * k_new.astype(BF16).astype(F32), axis=-1, keepdims=True)
            m_old = m_s[g]
            m_f = jnp.maximum(m_old, s_new)
            a = jnp.exp(m_old - m_f)
            pn = jnp.exp(s_new - m_f)
            o_slc = (a * acc_s[g] + pn * v_new.astype(BF16).astype(F32)) / (a * l_s[g] + pn)
            wk = _slab(win_ref, g, WIN_SLABS).astype(BF16)
            wv = _slab(win_ref, A_KV + g, WIN_SLABS).astype(BF16)
            win0 = CB_WIN * HEAD_DIM
            wk_new = prow[:, win0 + g * HEAD_DIM:win0 + (g + 1) * HEAD_DIM]
            wv_new = prow[:, win0 + gw + g * HEAD_DIM:win0 + gw + (g + 1) * HEAD_DIM]
            sw = lax.dot_general(q3b, wk, NT_DIMS, preferred_element_type=F32) - sl_ref[g] * wd
            sw = jnp.where(wok, sw, NEG)
            sw_new = jnp.sum(q3b.astype(F32) * wk_new.astype(BF16).astype(F32), axis=-1, keepdims=True)
            mw = jnp.maximum(jnp.max(sw, axis=-1, keepdims=True), sw_new)
            pw = jnp.where(wok, jnp.exp(sw - mw), 0.0)
            pwn = jnp.exp(sw_new - mw)
            o_win = ((jnp.dot(pw.astype(BF16), wv, preferred_element_type=F32) + pwn * wv_new.astype(BF16).astype(F32))
                     / (jnp.sum(pw, axis=-1, keepdims=True) + pwn))
            for r in range(A_REP):
                hd = g * A_REP + r
                gi = hd * N_GATES
                oc = ocmp[:, hd * HEAD_DIM:(hd + 1) * HEAD_DIM]
                outs.append(gt[:, gi:gi + 1] * oc + gt[:, gi + 1:gi + 2] * o_slc[r:r + 1]
                            + gt[:, gi + 2:gi + 3] * o_win[r:r + 1])
        o_ref[...] = jnp.concatenate(outs, axis=1)
        u = jax.nn.gelu(prow[:, CB_CU * LANE:CB_CU * LANE + C_WIDTH])
        v = _ln_rows(jax.nn.gelu(prow[:, CB_CV * LANE:CB_CV * LANE + C_WIDTH]), sg_ref[...], sb_ref[...])
        vrow_ref[...] = v
        osgu_ref[...] = u * (w00_ref[...] * v + b0_ref[...])


def _sample_nsa(page_table, cache_nsa4, prow, km, win_l, ocmp, sl_cols, sgu_g, sgu_b, w00, b0, layer, pg, p_len):
    n_s, n_pages = page_table.shape
    rows = cache_nsa4.shape[2]
    page = rows // NSA_SLABS
    n_win = win_l.shape[2] // WIN_SLABS
    nc = n_pages // pg
    one = lambda w: pl.BlockSpec((None, 1, w), lambda b, c, pt: (b, 0, 0))
    lvec = pl.BlockSpec((None, 1, C_WIDTH), lambda b, c, pt: (layer, 0, 0))
    return pl.pallas_call(
        functools.partial(_nsa_decode_kernel, pg=pg, p_len=p_len, n_win=n_win),
        grid_spec=pltpu.PrefetchScalarGridSpec(
            num_scalar_prefetch=1, grid=(n_s, nc),
            in_specs=_page_specs(pg, rows, layer) + [
                one(N_PROJ),
                pl.BlockSpec((None, 8, pg * page), lambda b, c, pt: (b, 0, c)),
                pl.BlockSpec((None, None, n_win * WIN_SLABS, HEAD_DIM), lambda b, c, pt: (layer, b, 0, 0)),
                one(A_HEADS * HEAD_DIM),
                pl.BlockSpec((A_KV, 8, 1), lambda b, c, pt: (0, 0, 0)),
                lvec, lvec, lvec, lvec],
            out_specs=[one(A_HEADS * HEAD_DIM), one(C_WIDTH), one(C_WIDTH)],
            scratch_shapes=[pltpu.VMEM((A_KV, 8, 1), F32), pltpu.VMEM((A_KV, 8, 1), F32),
                            pltpu.VMEM((A_KV, 8, HEAD_DIM), F32)]),
        out_shape=[jax.ShapeDtypeStruct((n_s, 1, A_HEADS * HEAD_DIM), F32),
                   jax.ShapeDtypeStruct((n_s, 1, C_WIDTH), F32), jax.ShapeDtypeStruct((n_s, 1, C_WIDTH), F32)],
        compiler_params=_cparams(("arbitrary", "arbitrary"), 40), name="sample_nsa")(
            page_table, *([cache_nsa4] * pg), prow, km, win_l, ocmp, sl_cols, sgu_g, sgu_b, w00, b0)


def _diff_decode_kernel(pt_ref, *refs, pg, p_len, lam_init):
    del pt_ref
    pages = refs[:pg]
    p_ref, g_ref, lam_ref, o_ref, m_s, l_s, acc_s = refs[pg:]
    c = pl.program_id(1)
    page = pages[0].shape[0] // DIFF_SLABS

    @pl.when(c == 0)
    def _():
        m_s[...] = jnp.full_like(m_s, NEG)
        l_s[...] = jnp.zeros_like(l_s)
        acc_s[...] = jnp.zeros_like(acc_s)

    prow = p_ref[...]
    rowi = lax.broadcasted_iota(I32, (8, HEAD_DIM), 0)
    half = lax.shift_right_logical(lax.broadcasted_iota(I32, (8, HEAD_DIM), 1), int(math.log2(B_HALF)))
    kpos = c * (pg * page) + lax.broadcasted_iota(I32, (8, pg * page), 1)
    dist = (p_len - kpos).astype(F32)

    def q_rows(h):
        c0 = (CB_BQ + h) * LANE
        q = jnp.broadcast_to(prow[:, c0:c0 + HEAD_DIM], (8, HEAD_DIM)) * (B_HALF ** -0.5)
        return jnp.where(half == rowi, q, 0.0).astype(BF16)

    for h in range(B_HEADS):
        qh = q_rows(h)
        s = jnp.concatenate([lax.dot_general(qh, _slab(pages[i], 2 * h, DIFF_SLABS).astype(BF16), NT_DIMS,
                                             preferred_element_type=F32) for i in range(pg)], axis=1)
        s = s - float(SL_B[h]) * dist
        vbs = [_slab(pages[i], 2 * h + 1, DIFF_SLABS).astype(BF16) for i in range(pg)]
        _decode_update(s, None, vbs, m_s.at[h], l_s.at[h], acc_s.at[h], page)

    @pl.when(c == pl.num_programs(1) - 1)
    def _():
        outs = []
        for h in range(B_HEADS):
            qh = q_rows(h).astype(F32)
            k_new = prow[:, (CB_BK + h) * LANE:(CB_BK + h + 1) * LANE].astype(BF16).astype(F32)
            v_new = prow[:, (CB_BV + h) * LANE:(CB_BV + h + 1) * LANE].astype(BF16).astype(F32)
            s_new = jnp.sum(qh * k_new, axis=-1, keepdims=True)
            m_old = m_s[h]
            m_f = jnp.maximum(m_old, s_new)
            a = jnp.exp(m_old - m_f)
            pn = jnp.exp(s_new - m_f)
            o = (a * acc_s[h] + pn * v_new) / (a * l_s[h] + pn)
            oh = o[0:1] - lam_ref[0] * o[1:2]
            oh = oh * lax.rsqrt(jnp.mean(oh * oh, axis=-1, keepdims=True) + RMS_EPS) * g_ref[...] * (1.0 - lam_init)
            outs.append(oh)
        o_ref[...] = jnp.concatenate(outs, axis=1)


def _sample_diff(page_table, lam, cache_diff4, prow, diff_g, layer, pg, p_len, lam_init):
    n_s, n_pages = page_table.shape
    rows = cache_diff4.shape[2]
    hw = B_HEADS * HEAD_DIM
    nc = n_pages // pg
    return pl.pallas_call(
        functools.partial(_diff_decode_kernel, pg=pg, p_len=p_len, lam_init=lam_init),
        grid_spec=pltpu.PrefetchScalarGridSpec(
            num_scalar_prefetch=1, grid=(n_s, nc),
            in_specs=_page_specs(pg, rows, layer) + [
                pl.BlockSpec((None, 1, N_PROJ), lambda b, c, pt: (b, 0, 0)),
                pl.BlockSpec((None, 1, HEAD_DIM), lambda b, c, pt: (layer, 0, 0)),
                pl.BlockSpec(memory_space=pltpu.SMEM)],
            out_specs=pl.BlockSpec((None, 1, hw), lambda b, c, pt: (b, 0, 0)),
            scratch_shapes=[pltpu.VMEM((B_HEADS, 8, 1), F32), pltpu.VMEM((B_HEADS, 8, 1), F32),
                            pltpu.VMEM((B_HEADS, 8, HEAD_DIM), F32)]),
        out_shape=jax.ShapeDtypeStruct((n_s, 1, hw), F32),
        compiler_params=_cparams(("arbitrary", "arbitrary"), 40), name="sample_diff")(
            page_table, *([cache_diff4] * pg), prow, diff_g, lam)


def kernel(x_prompt, x_sample, cache_nsa, cache_diff, state_win, page_table, ln_in_g, ln_in_b, w_in, cmp_wk, cmp_wv,
           lam_q1, lam_k1, lam_q2, lam_k2, diff_norm_g, sgu_ln_g, sgu_ln_b, sgu_w, sgu_b, w_out, ln1_g, ln1_b,
           router_w, router_b, w_up, b_up, w_down, b_down, ln2_g, ln2_b):
    n_b, t_len, d = x_prompt.shape
    n_s, t_s, _ = x_sample.shape
    depth = w_in.shape[0]
    assert t_s == 1 and t_len % CHUNK == 0
    n_pages = page_table.shape[1]
    page = cache_nsa.shape[2]
    p_len = n_pages * page
    n_win = state_win.shape[2]
    assert p_len % CHUNK == 0 and page % NSA_BLOCK == 0 and n_win == min(WINDOW, p_len)
    m_p = n_b * t_len
    n_valid = m_p + n_s
    mt = -(-n_valid // 256) * 256
    alpha = (2 * depth) ** 0.25
    pg = _pick_tile(n_pages, (8, 4, 2, 1))
    bpc = pg * (page // NSA_BLOCK)

    w_in_r = jnp.concatenate(
        [w_in[:, :, :GATE_ORIG], w_in[:, :, GATE_ORIG + N_GATE_COLS:], w_in[:, :, GATE_ORIG:GATE_ORIG + N_GATE_COLS],
         jnp.zeros((depth, d, LANE - N_GATE_COLS), w_in.dtype)], axis=-1).astype(BF16)
    assert w_in_r.shape[-1] == N_PROJ
    w_out_b = w_out.astype(BF16)
    p_even = jnp.asarray(np.arange(2 * LANE)[:, None] == 2 * np.arange(LANE)[None, :], BF16)
    mw = {"w_up": w_up, "b_up": b_up[:, :, None, :], "w_down": w_down, "b_down": b_down[:, :, None, :],
          "p_even": p_even}
    rw_pad = jnp.pad(router_w, ((0, 0), (0, 0), (0, LANE - N_EXPERTS)))
    rb_pad = jnp.pad(router_b, ((0, 0), (0, LANE - N_EXPERTS)), constant_values=NEG)[:, None, :]
    ln1_g3, ln1_b3, ln2_g3, ln2_b3 = (a[:, None, :] for a in (ln1_g, ln1_b, ln2_g, ln2_b))
    sgu_g3, sgu_b3 = sgu_ln_g[:, None, :], sgu_ln_b[:, None, :]
    sgu_bt = jnp.swapaxes(sgu_b, 1, 2)
    sgu_w00 = jnp.repeat(sgu_w[:, :, 0, 0], HEAD_DIM, axis=1)[:, None, :]
    sgu_b0 = jnp.repeat(sgu_b[:, :, 0], HEAD_DIM, axis=1)[:, None, :]
    diff_g3 = diff_norm_g[:, None, :]
    nb_p = t_len // NSA_BLOCK
    e_prompt = jnp.asarray((np.arange(t_len)[None, :] // NSA_BLOCK == np.arange(nb_p)[:, None]), BF16)
    e_small = jnp.asarray((np.arange(bpc * NSA_BLOCK)[None, :] // NSA_BLOCK == np.arange(p_len // NSA_BLOCK)[:, None] % bpc), BF16)
    sl_a_cols = jnp.asarray(np.concatenate([SL_A, np.zeros((A_KV, 8 - A_REP), np.float32)], axis=1)[:, :, None])
    cache_nsa4 = cache_nsa.reshape(depth, cache_nsa.shape[1], page * NSA_SLABS, HEAD_DIM)
    cache_diff4 = jnp.swapaxes(cache_diff, 3, 4).reshape(depth, cache_diff.shape[1], page * DIFF_SLABS, HEAD_DIM)
    win4 = state_win.reshape(depth, n_s, n_win * WIN_SLABS, HEAD_DIM)

    x_all = jnp.concatenate([x_prompt.reshape(m_p, d), x_sample.reshape(n_s, d),
                             jnp.zeros((mt - n_valid, d), x_prompt.dtype)], axis=0)
    xf, xb = _ln_in(x_all, ln_in_g, ln_in_b)

    nsa_p, nsa_s, diff_p, diff_s, win_p, win_s, sgu_s = [], [], [], [], [], [], []
    for l in range(depth):
        lam_init = 0.8 - 0.6 * math.exp(-0.3 * l)
        lam = (jnp.exp(jnp.sum(lam_q1[l] * lam_k1[l])) - jnp.exp(jnp.sum(lam_q2[l] * lam_k2[l])) + lam_init)
        lam = lam.astype(F32).reshape(1)
        proj = _proj(xb, w_in_r, l)
        prow = proj[m_p:n_valid].reshape(n_s, 1, N_PROJ)

        o_nsa = _nsa_prompt(proj, cmp_wk, cmp_wv, e_prompt, l, n_b, t_len)
        o_diff = _diff_prompt(lam, proj, diff_g3, l, n_b, t_len, lam_init)
        o_sgu = _sgu_prompt(proj, sgu_g3, sgu_b3, sgu_w, sgu_bt, l, m_p)

        kvcmp = _sample_pool(page_table, cache_nsa4, cmp_wk, cmp_wv, l, pg)
        ocmp_s, km = _sample_select(prow, kvcmp, sl_a_cols, e_small, p_len, bpc)
        o_nsa_s, o_sgu_s, v_rows = _sample_nsa(page_table, cache_nsa4, prow, km, win4, ocmp_s, sl_a_cols,
                                               sgu_g3, sgu_b3, sgu_w00, sgu_b0, l, pg, p_len)
        o_diff_s = _sample_diff(page_table, lam, cache_diff4, prow, diff_g3, l, pg, p_len, lam_init)

        pad = lambda w: jnp.zeros((mt - n_valid, w), BF16)
        hn = jnp.concatenate([o_nsa, o_nsa_s.reshape(n_s, -1).astype(BF16), pad(o_nsa.shape[1])], axis=0)
        hd = jnp.concatenate([o_diff, o_diff_s.reshape(n_s, -1).astype(BF16), pad(o_diff.shape[1])], axis=0)
        hs = jnp.concatenate([o_sgu, o_sgu_s.reshape(n_s, -1).astype(BF16), pad(o_sgu.shape[1])], axis=0)
        x1, logits = _outproj(hn, hd, hs, w_out_b, xf, ln1_g3, ln1_b3, rw_pad, rb_pad, l, alpha)
        xf, xb = _moe_layer(x1, logits, n_valid, mw, ln2_g3, ln2_b3, l, alpha)

        c0, c1 = CB_NSA * LANE, CB_WIN * LANE
        nsa_p.append(proj[:m_p, c0:c1].reshape(n_b, t_len, 4, A_KV, HEAD_DIM))
        nsa_s.append(proj[m_p:n_valid, c0:c1].reshape(n_s, 1, 4, A_KV, HEAD_DIM))
        d0, d1 = CB_BK * LANE, CB_CU * LANE
        diff_p.append(proj[:m_p, d0:d1].reshape(n_b, t_len, 2, B_HEADS, HEAD_DIM))
        diff_s.append(proj[m_p:n_valid, d0:d1].reshape(n_s, 1, 2, B_HEADS, HEAD_DIM))
        w_rows = proj[:m_p, c1:CB_BQ * LANE].reshape(n_b, t_len, 2, A_KV, HEAD_DIM)
        win_p.append(w_rows[:, t_len - min(WINDOW, t_len):])
        w_new = proj[m_p:n_valid, c1:CB_BQ * LANE].reshape(n_s, 1, 2, A_KV, HEAD_DIM)
        win_s.append(jnp.concatenate([state_win[l], w_new], axis=1)[:, 1:])
        sgu_s.append(v_rows.reshape(n_s, 1, C_WIDTH))

    y_p = xf[:m_p].reshape(n_b, t_len, d)
    y_s = xf[m_p:n_valid].reshape(n_s, 1, d)
    return (y_p, y_s, jnp.stack(nsa_p), jnp.stack(nsa_s), jnp.stack(diff_p), jnp.stack(diff_s),
            jnp.stack(win_p), jnp.stack(win_s), jnp.stack(sgu_s))
```

```python
import functools
import math

import numpy as np
import jax
import jax.numpy as jnp
from jax import lax
from jax.experimental import pallas as pl
from jax.experimental.pallas import tpu as pltpu

F32 = jnp.float32
BF16 = jnp.bfloat16
I32 = jnp.int32

HEAD_DIM = 128
A_KV = 2
A_REP = 3
A_HEADS = A_KV * A_REP
B_HEADS = 6
B_HALF = HEAD_DIM // 2
C_GROUPS = 4
C_WIDTH = C_GROUPS * HEAD_DIM
CHUNK = 128
NSA_BLOCK = 64
N_SELECT = 16
WINDOW = 512
N_GATES = 3
N_EXPERTS = 32
TOP_K = 4
SWIGLU_LIMIT = 7.0
SWIGLU_ALPHA = 1.702
LN_EPS = 1e-5
RMS_EPS = 1e-5
LANE = 128
NEG = -1e30

_SL = (2.0 ** (-8.0 * np.arange(1, A_HEADS + B_HEADS + 1) / (A_HEADS + B_HEADS))).astype(np.float32)
SL_A = _SL[0::2].reshape(A_KV, A_REP)
SL_B = _SL[1::2]

CB_AQ = 0
CB_NSA = 6
CB_WIN = 14
CB_BQ = 18
CB_BK = 24
CB_BV = 30
CB_CU = 36
CB_CV = 40
CB_GATE = 44
N_PROJ = 45 * LANE
N_GATE_COLS = A_HEADS * N_GATES
GATE_ORIG = 2304

NT_DIMS = (((1,), (1,)), ((), ()))
PROMPT_TK = (512, 256, 128)


def _cparams(sem, vmem_mb=None):
    kw = dict(dimension_semantics=sem)
    if vmem_mb is not None:
        kw["vmem_limit_bytes"] = vmem_mb * 1024 * 1024
    return pltpu.CompilerParams(**kw)


def _pick_tile(n, candidates):
    for c in candidates:
        if n % c == 0:
            return c
    return n


def _ln_rows(x, g, b):
    mu = jnp.mean(x, axis=-1, keepdims=True)
    xc = x - mu
    var = jnp.mean(xc * xc, axis=-1, keepdims=True)
    return xc * lax.rsqrt(var + LN_EPS) * g + b


def _pack_bf16_pairs(y):
    n = y.shape[1] // 2
    lo = pltpu.bitcast(y[:, :n].astype(BF16).astype(F32), jnp.uint32)
    hi = pltpu.bitcast(y[:, n:].astype(BF16).astype(F32), jnp.uint32)
    return (hi & jnp.uint32(0xFFFF0000)) | (lo >> 16)


def _unpack_bf16_pairs(w):
    lo = pltpu.bitcast(w << 16, F32)
    hi = pltpu.bitcast(w & jnp.uint32(0xFFFF0000), F32)
    return jnp.concatenate([lo, hi], axis=1).astype(BF16)


def _select_by_index(idx, values):
    out = values[-1]
    for j in range(len(values) - 2, -1, -1):
        out = jnp.where(idx == j, values[j], out)
    return out


def _ln_in_kernel(x_ref, g_ref, b_ref, of_ref, ob_ref):
    y = _ln_rows(x_ref[...], g_ref[...], b_ref[...])
    of_ref[...] = y
    ob_ref[...] = y.astype(BF16)


def _ln_in(x, g, b):
    mt, d = x.shape
    tm = _pick_tile(mt, (256, 128, 8))
    row = pl.BlockSpec((tm, d), lambda i: (i, 0))
    vec = pl.BlockSpec((1, d), lambda i: (0, 0))
    return pl.pallas_call(
        _ln_in_kernel, grid=(mt // tm,), in_specs=[row, vec, vec], out_specs=[row, row],
        out_shape=[jax.ShapeDtypeStruct((mt, d), F32), jax.ShapeDtypeStruct((mt, d), BF16)],
        compiler_params=_cparams(("arbitrary",)), name="ln_in")(x, g.reshape(1, d), b.reshape(1, d))


def _proj_kernel(x_ref, w_ref, o_ref):
    o_ref[...] = jnp.dot(x_ref[...], w_ref[...], preferred_element_type=F32)


def _proj(xb, w_all, layer):
    mt, d = xb.shape
    n = w_all.shape[-1]
    tm = _pick_tile(mt, (768, 512, 256, 128, 8))
    tn = _pick_tile(n, (1152, 640, 128))
    return pl.pallas_call(
        _proj_kernel, grid=(n // tn, mt // tm),
        in_specs=[pl.BlockSpec((tm, d), lambda j, i: (i, 0)),
                  pl.BlockSpec((None, d, tn), lambda j, i: (layer, 0, j))],
        out_specs=pl.BlockSpec((tm, tn), lambda j, i: (i, j)),
        out_shape=jax.ShapeDtypeStruct((mt, n), F32),
        compiler_params=_cparams(("arbitrary", "arbitrary"), 48), name="in_proj")(xb, w_all)


def _outproj_kernel(hn_ref, hd_ref, hs_ref, wn_ref, wd_ref, ws_ref, x_ref, g_ref, b_ref, rw_ref, rb_ref,
                    xf_ref, xp_ref, lg_ref, *, alpha):
    h = jnp.dot(hn_ref[...], wn_ref[...], preferred_element_type=F32)
    h = h + jnp.dot(hd_ref[...], wd_ref[...], preferred_element_type=F32)
    h = h + jnp.dot(hs_ref[...], ws_ref[...], preferred_element_type=F32)
    y = _ln_rows(alpha * x_ref[...] + h, g_ref[...], b_ref[...])
    xf_ref[...] = y
    xp_ref[...] = _pack_bf16_pairs(y)
    yh = y.astype(BF16)
    yl = (y - yh.astype(F32)).astype(BF16)
    rw = rw_ref[...]
    wh = rw.astype(BF16)
    wl = (rw - wh.astype(F32)).astype(BF16)
    lg = jnp.dot(yh, wh, preferred_element_type=F32)
    lg = lg + jnp.dot(yh, wl, preferred_element_type=F32)
    lg = lg + jnp.dot(yl, wh, preferred_element_type=F32)
    lg_ref[...] = lg + rb_ref[...]


def _outproj(hn, hd, hs, w_out_b, x, g, b, rw_pad, rb_pad, layer, alpha):
    mt, d = x.shape
    tm = _pick_tile(mt, (256, 128, 8))
    wn, wdw, wsw = hn.shape[1], hd.shape[1], hs.shape[1]
    assert wn == wdw and (wn + wdw) % wsw == 0
    row = lambda w: pl.BlockSpec((tm, w), lambda i: (i, 0))
    vec = pl.BlockSpec((None, 1, d), lambda i: (layer, 0, 0))
    return pl.pallas_call(
        functools.partial(_outproj_kernel, alpha=alpha), grid=(mt // tm,),
        in_specs=[row(wn), row(wdw), row(wsw),
                  pl.BlockSpec((None, wn, d), lambda i: (layer, 0, 0)),
                  pl.BlockSpec((None, wdw, d), lambda i: (layer, 1, 0)),
                  pl.BlockSpec((None, wsw, d), lambda i: (layer, (wn + wdw) // wsw, 0)),
                  row(d), vec, vec,
                  pl.BlockSpec((None, d, LANE), lambda i: (layer, 0, 0)),
                  pl.BlockSpec((None, 1, LANE), lambda i: (layer, 0, 0))],
        out_specs=[row(d), row(d // 2), row(LANE)],
        out_shape=[jax.ShapeDtypeStruct((mt, d), F32), jax.ShapeDtypeStruct((mt, d // 2), jnp.uint32),
                   jax.ShapeDtypeStruct((mt, LANE), F32)],
        compiler_params=_cparams(("arbitrary",), 48), name="out_proj_ln")(
            hn, hd, hs, w_out_b, w_out_b, w_out_b, x, g, b, rw_pad, rb_pad)


def _route_kernel(lg_ref, e_ref, r_ref, g_ref, cnt_ref, carry, *, tb, n_valid):
    i = pl.program_id(0)

    @pl.when(i == 0)
    def _():
        carry[...] = jnp.zeros_like(carry)

    work = lg_ref[...]
    lane = lax.broadcasted_iota(I32, (tb, LANE), 1)
    lane_f = lane.astype(F32)
    row = i * tb + lax.broadcasted_iota(I32, (tb, 1), 0)
    valid = row < n_valid
    vals, idxs = [], []
    for _ in range(TOP_K):
        m = jnp.max(work, axis=-1, keepdims=True)
        idx = jnp.min(jnp.where(work == m, lane_f, float(LANE)), axis=-1, keepdims=True)
        vals.append(m)
        idxs.append(idx)
        work = jnp.where(lane_f == idx, -jnp.inf, work)
    ex = [jnp.exp(v - vals[0]) for v in vals]
    den = ex[0] + ex[1] + ex[2] + ex[3]
    hot = [(lane_f == idx) & valid for idx in idxs]
    a = jnp.zeros((tb, LANE), F32)
    for hk in hot:
        a = a + jnp.where(hk, 1.0, 0.0)
    tri = (lax.broadcasted_iota(I32, (tb, tb), 0) > lax.broadcasted_iota(I32, (tb, tb), 1))
    before = jnp.dot(jnp.where(tri, 1.0, 0.0).astype(BF16), a.astype(BF16), preferred_element_type=F32) + carry[...]
    e_out = jnp.zeros((tb, LANE), F32)
    r_out = jnp.zeros((tb, LANE), F32)
    g_out = jnp.zeros((tb, LANE), F32)
    for k in range(TOP_K):
        rank = jnp.sum(jnp.where(hot[k], before, 0.0), axis=-1, keepdims=True)
        e_out = jnp.where(lane == k, idxs[k], e_out)
        r_out = jnp.where(lane == k, rank, r_out)
        g_out = jnp.where(lane == k, jnp.where(valid, ex[k] / den, 0.0), g_out)
    e_ref[...] = e_out.astype(I32)
    r_ref[...] = r_out.astype(I32)
    g_ref[...] = g_out
    carry[...] = carry[...] + jnp.sum(a, axis=0, keepdims=True)
    cnt_ref[...] = carry[...]


def _route(logits, n_valid):
    mt = logits.shape[0]
    tb = _pick_tile(mt, (256, 128, 8))
    row = pl.BlockSpec((tb, LANE), lambda i: (i, 0))
    return pl.pallas_call(
        functools.partial(_route_kernel, tb=tb, n_valid=n_valid), grid=(mt // tb,),
        in_specs=[row], out_specs=[row, row, row, pl.BlockSpec((1, LANE), lambda i: (0, 0))],
        out_shape=[jax.ShapeDtypeStruct((mt, LANE), I32), jax.ShapeDtypeStruct((mt, LANE), I32),
                   jax.ShapeDtypeStruct((mt, LANE), F32), jax.ShapeDtypeStruct((1, LANE), F32)],
        scratch_shapes=[pltpu.VMEM((1, LANE), F32)],
        compiler_params=_cparams(("arbitrary",)), name="moe_route")(logits)


def _scatter_kernel(dest_ref, x_ref, xs_in, xs_out, sem, *, ts):
    del xs_in
    i = pl.program_id(0)

    def issue(t, c):
        for k in range(TOP_K):
            d = dest_ref[(i * ts + t) * TOP_K + k]
            pltpu.make_async_copy(x_ref.at[pl.ds(t, 1)], xs_out.at[pl.ds(d, 1)], sem).start()
        return c

    lax.fori_loop(0, ts, issue, 0)

    def drain(t, c):
        for k in range(TOP_K):
            pltpu.make_async_copy(x_ref.at[pl.ds(0, 1)], xs_out.at[pl.ds(0, 1)], sem).wait()
        return c

    lax.fori_loop(0, ts, drain, 0)


def _scatter_rows(dest_flat, x, xs_zero):
    mt, d = x.shape
    ts = _pick_tile(mt, (256, 128, 8))
    return pl.pallas_call(
        functools.partial(_scatter_kernel, ts=ts),
        grid_spec=pltpu.PrefetchScalarGridSpec(
            num_scalar_prefetch=1, grid=(mt // ts,),
            in_specs=[pl.BlockSpec((ts, d), lambda i, dest: (i, 0)), pl.BlockSpec(memory_space=pl.ANY)],
            out_specs=pl.BlockSpec(memory_space=pl.ANY),
            scratch_shapes=[pltpu.SemaphoreType.DMA]),
        out_shape=jax.ShapeDtypeStruct(xs_zero.shape, xs_zero.dtype),
        input_output_aliases={2: 0},
        compiler_params=_cparams(("arbitrary",)), name="moe_scatter")(dest_flat, x, xs_zero)


def _weights_changed(be_ref):
    b = pl.program_id(1)
    return (b == 0) | (be_ref[b] != be_ref[jnp.maximum(b - 1, 0)])


def _moe_up_kernel(be_ref, na_ref, x_ref, w_ref, b_ref, p_ref, o_ref, wb_s):
    active = pl.program_id(1) < na_ref[0]

    @pl.when(active & _weights_changed(be_ref))
    def _():
        wb_s[...] = w_ref[...].astype(BF16)

    @pl.when(active)
    def _():
        h = jnp.dot(_unpack_bf16_pairs(x_ref[...]), wb_s[...], preferred_element_type=F32) + b_ref[...]
        glu = jnp.minimum(h, SWIGLU_LIMIT)
        gated = glu * jax.nn.sigmoid(SWIGLU_ALPHA * glu)
        lin1 = jnp.clip(h, -SWIGLU_LIMIT, SWIGLU_LIMIT) + 1.0
        for k in range(o_ref.shape[1] // LANE):
            pieces = []
            for c0 in (2 * k * LANE, (2 * k + 1) * LANE):
                nxt = pltpu.roll(lin1[:, c0:c0 + LANE], LANE - 1, 1)
                pieces.append((gated[:, c0:c0 + LANE] * nxt).astype(BF16))
            prod = jnp.concatenate(pieces, axis=1)
            o_ref[:, k * LANE:(k + 1) * LANE] = jnp.dot(prod, p_ref[...], preferred_element_type=F32).astype(BF16)

    @pl.when(jnp.logical_not(active))
    def _():
        o_ref[...] = jnp.zeros_like(o_ref)


def _moe_up(block_e, n_act, xs, w_up, b_up4, p_even, layer, n_blocks, tme):
    d = w_up.shape[-2]
    f2 = w_up.shape[-1]
    tn2 = _pick_tile(f2, (1024, 512, 256))
    rows = lambda j, b, be, na: (jnp.minimum(b, na[0] - 1), 0)
    return pl.pallas_call(
        _moe_up_kernel,
        grid_spec=pltpu.PrefetchScalarGridSpec(
            num_scalar_prefetch=2, grid=(f2 // tn2, n_blocks),
            in_specs=[pl.BlockSpec((tme, xs.shape[1]), rows),
                      pl.BlockSpec((None, None, d, tn2), lambda j, b, be, na: (layer, be[b], 0, j)),
                      pl.BlockSpec((None, None, 1, tn2), lambda j, b, be, na: (layer, be[b], 0, j)),
                      pl.BlockSpec((2 * LANE, LANE), lambda j, b, be, na: (0, 0))],
            out_specs=pl.BlockSpec((tme, tn2 // 2), lambda j, b, be, na: (b, j)),
            scratch_shapes=[pltpu.VMEM((d, tn2), BF16)]),
        out_shape=jax.ShapeDtypeStruct((n_blocks * tme, f2 // 2), BF16),
        compiler_params=_cparams(("arbitrary", "arbitrary"), 48), name="moe_up")(
            block_e, n_act, xs, w_up, b_up4, p_even)


def _moe_down_kernel(be_ref, na_ref, a_ref, w_ref, b_ref, o_ref, wb_s):
    active = pl.program_id(1) < na_ref[0]

    @pl.when(active & _weights_changed(be_ref))
    def _():
        wb_s[...] = w_ref[...].astype(BF16)

    @pl.when(active)
    def _():
        o_ref[...] = jnp.dot(a_ref[...], wb_s[...], preferred_element_type=F32) + b_ref[...]

    @pl.when(jnp.logical_not(active))
    def _():
        o_ref[...] = jnp.zeros_like(o_ref)


def _moe_down(block_e, n_act, act, w_down, b_down4, layer, n_blocks, tme):
    f = act.shape[1]
    d = w_down.shape[-1]
    tn = _pick_tile(d, (1024, 512, 256, 128))
    rows = lambda j, b, be, na: (jnp.minimum(b, na[0] - 1), 0)
    return pl.pallas_call(
        _moe_down_kernel,
        grid_spec=pltpu.PrefetchScalarGridSpec(
            num_scalar_prefetch=2, grid=(d // tn, n_blocks),
            in_specs=[pl.BlockSpec((tme, f), rows),
                      pl.BlockSpec((None, None, f, tn), lambda j, b, be, na: (layer, be[b], 0, j)),
                      pl.BlockSpec((None, None, 1, tn), lambda j, b, be, na: (layer, be[b], 0, j))],
            out_specs=pl.BlockSpec((tme, tn), lambda j, b, be, na: (b, j)),
            scratch_shapes=[pltpu.VMEM((f, tn), BF16)]),
        out_shape=jax.ShapeDtypeStruct((n_blocks * tme, d), F32),
        compiler_params=_cparams(("arbitrary", "arbitrary"), 48), name="moe_down")(
            block_e, n_act, act, w_down, b_down4)


def _combine_kernel(dest_ref, y_hbm, x_ref, gate_ref, g_ref, b_ref, xf_ref, xb_ref, buf, sem, *, tc, alpha):
    i = pl.program_id(0)

    def issue(t, c):
        for k in range(TOP_K):
            d = dest_ref[(i * tc + t) * TOP_K + k]
            pltpu.make_async_copy(y_hbm.at[pl.ds(d, 1)], buf.at[k, pl.ds(t, 1)], sem).start()
        return c

    lax.fori_loop(0, tc, issue, 0)

    def drain(t, c):
        for k in range(TOP_K):
            pltpu.make_async_copy(y_hbm.at[pl.ds(0, 1)], buf.at[k, pl.ds(0, 1)], sem).wait()
        return c

    lax.fori_loop(0, tc, drain, 0)
    gate = gate_ref[...]
    y = gate[:, 0:1] * buf[0]
    for k in range(1, TOP_K):
        y = y + gate[:, k:k + 1] * buf[k]
    out = _ln_rows(alpha * x_ref[...] + y, g_ref[...], b_ref[...])
    xf_ref[...] = out
    xb_ref[...] = out.astype(BF16)


def _combine(dest_flat, y_rows, x1, gates, g, b, layer, alpha):
    mt, d = x1.shape
    tc = _pick_tile(mt, (128, 8))
    row = lambda w: pl.BlockSpec((tc, w), lambda i, dest: (i, 0))
    vec = pl.BlockSpec((None, 1, d), lambda i, dest: (layer, 0, 0))
    return pl.pallas_call(
        functools.partial(_combine_kernel, tc=tc, alpha=alpha),
        grid_spec=pltpu.PrefetchScalarGridSpec(
            num_scalar_prefetch=1, grid=(mt // tc,),
            in_specs=[pl.BlockSpec(memory_space=pl.ANY), row(d), row(LANE), vec, vec],
            out_specs=[row(d), row(d)],
            scratch_shapes=[pltpu.VMEM((TOP_K, tc, d), F32), pltpu.SemaphoreType.DMA]),
        out_shape=[jax.ShapeDtypeStruct((mt, d), F32), jax.ShapeDtypeStruct((mt, d), BF16)],
        compiler_params=_cparams(("arbitrary",), 32), name="moe_combine_ln")(dest_flat, y_rows, x1, gates, g, b)


def _moe_layer(x1, x1_packed, logits, n_valid, mw, ln_g, ln_b, layer, alpha):
    mt, d = x1.shape
    tme = 256
    n_blocks = -(-(n_valid * TOP_K) // tme) + N_EXPERTS
    e_i, r_i, gates, counts = _route(logits, n_valid)
    cnt = counts[0, :N_EXPERTS].astype(I32)
    padded = (cnt + tme - 1) // tme * tme
    pend = jnp.cumsum(padded)
    pstart = pend - padded
    n_act = (pend[-1] // tme).astype(I32)
    blk = jnp.arange(n_blocks, dtype=I32)
    block_e = jnp.minimum(jnp.sum((blk[:, None] * tme >= pend[None, :]).astype(I32), axis=1), N_EXPERTS - 1)
    block_e = jnp.where(blk < n_act, block_e, block_e[jnp.maximum(n_act - 1, 0)]).astype(I32)
    e4 = e_i[:, :TOP_K]
    dest = pstart[jnp.clip(e4, 0, N_EXPERTS - 1)] + r_i[:, :TOP_K]
    tok = jnp.arange(mt, dtype=I32)[:, None]
    valid = tok < n_valid
    trash = n_blocks * tme + (tok - n_valid) * TOP_K + jnp.arange(TOP_K, dtype=I32)[None, :]
    dest_scatter = jnp.where(valid, dest, trash).astype(I32).reshape(-1)
    dest_gather = jnp.where(valid, dest, 0).astype(I32).reshape(-1)
    r_tot = n_blocks * tme + (mt - n_valid) * TOP_K
    r_tot = -(-r_tot // 8) * 8
    xs = _scatter_rows(dest_scatter, x1_packed, jnp.zeros((r_tot, x1_packed.shape[1]), x1_packed.dtype))
    n_act1 = n_act.reshape(1)
    act = _moe_up(block_e, n_act1, xs, mw["w_up"], mw["b_up"], mw["p_even"], layer, n_blocks, tme)
    y_rows = _moe_down(block_e, n_act1, act, mw["w_down"], mw["b_down"], layer, n_blocks, tme)
    return _combine(dest_gather, y_rows, x1, gates, ln_g, ln_b, layer, alpha)


def _flash_step(qb, kj, vj, bias_list, shift_list, m, l, acc):
    tq = bias_list[0].shape[0]
    s = lax.dot_general(qb, kj, NT_DIMS, preferred_element_type=F32)
    t = jnp.concatenate([s[r * tq:(r + 1) * tq] + b for r, b in enumerate(bias_list)], axis=0)
    shift = jnp.concatenate([jnp.full((tq, 1), 1.0, F32) * sh for sh in shift_list], axis=0)
    m_new = jnp.maximum(m, jnp.max(t, axis=-1, keepdims=True) + shift)
    a = jnp.exp(m - m_new)
    p = jnp.exp(t - (m_new - shift))
    l = a * l + jnp.sum(p, axis=-1, keepdims=True)
    acc = a * acc + jnp.dot(p.astype(BF16), vj, preferred_element_type=F32)
    return m_new, l, acc


def _nsa_kernel(q_ref, ck_ref, cv_ref, sk_ref, sv_ref, wk_ref, wv_ref, gt_ref, wpk_ref, wpv_ref, e_ref,
                o_ref, kcmp_s, vcmp_s, skb, svb, wkb, wvb, km_s, *, t_len, tq, tk):
    g = pl.program_id(1)
    i = pl.program_id(2)
    nb = t_len // NSA_BLOCK
    n_sel = min(N_SELECT, -(-t_len // NSA_BLOCK))

    @pl.when(i == 0)
    def _():
        ck = ck_ref[...].reshape(nb, NSA_BLOCK, HEAD_DIM)
        kcmp_s[...] = jnp.sum(ck * wpk_ref[...][None, :, :], axis=1).astype(BF16)
        cv = cv_ref[...].reshape(nb, NSA_BLOCK, HEAD_DIM)
        vcmp_s[...] = jnp.sum(cv * wpv_ref[...][None, :, :], axis=1).astype(BF16)
        skb[...] = sk_ref[...].astype(BF16)
        svb[...] = sv_ref[...].astype(BF16)
        wkb[...] = wk_ref[...].astype(BF16)
        wvb[...] = wv_ref[...].astype(BF16)

    s0 = i * tq
    slopes = [jnp.where(g == 0, float(SL_A[0][r]), float(SL_A[1][r])) for r in range(A_REP)]
    q = q_ref[...]
    q3b = (jnp.concatenate([q[:, r * HEAD_DIM:(r + 1) * HEAD_DIM] for r in range(A_REP)], axis=0)
           * (HEAD_DIM ** -0.5)).astype(BF16)
    qpos = s0 + lax.broadcasted_iota(I32, (tq, 1), 0)

    sc = lax.dot_general(q3b, kcmp_s[...], NT_DIMS, preferred_element_type=F32)
    blk = lax.broadcasted_iota(I32, (tq, nb), 1)
    endj = blk * NSA_BLOCK + (NSA_BLOCK - 1)
    okc = endj <= qpos
    distc = (qpos - endj).astype(F32)
    imp = jnp.zeros((tq, nb), F32)
    o_cmp = []
    for r in range(A_REP):
        sm = jnp.where(okc, sc[r * tq:(r + 1) * tq] - slopes[r] * distc, NEG)
        m = jnp.max(sm, axis=-1, keepdims=True)
        e = jnp.where(okc, jnp.exp(sm - m), 0.0)
        p = e / jnp.maximum(jnp.sum(e, axis=-1, keepdims=True), 1e-30)
        imp = imp + p
        o_cmp.append(jnp.dot(p.astype(BF16), vcmp_s[...], preferred_element_type=F32))

    cur = lax.shift_right_logical(qpos, int(math.log2(NSA_BLOCK)))
    score = jnp.where(blk == cur, jnp.inf, jnp.where(blk > cur, -jnp.inf, imp))
    cnt = jnp.zeros((tq, nb), F32)
    for c in range(nb):
        col = score[:, c:c + 1]
        beats = (col > score) | ((col == score) & (blk > c))
        cnt = cnt + jnp.where(beats, 1.0, 0.0)
    sel = jnp.where(cnt < n_sel, 1.0, 0.0).astype(BF16)
    km = jnp.dot(sel, e_ref[...], preferred_element_type=F32)
    for c in range(t_len // tk):
        km_s[c] = km[:, c * tk:(c + 1) * tk]

    rc = (lax.broadcasted_iota(I32, (tq, tk), 0) - lax.broadcasted_iota(I32, (tq, tk), 1)).astype(F32)
    bias0 = [-slopes[r] * rc for r in range(A_REP)]
    m0 = jnp.full((A_REP * tq, 1), NEG, F32)
    l0 = jnp.zeros((A_REP * tq, 1), F32)
    a0 = jnp.zeros((A_REP * tq, HEAD_DIM), F32)
    j_diag = s0 // tk

    def slc_step(j, carry, causal):
        off = pl.multiple_of(j * tk, tk)
        delta = (s0 - j * tk).astype(F32)
        ok = km_s[j] > 0.5
        if causal:
            ok = ok & (rc + delta >= 0.0)
        bias = [jnp.where(ok, bias0[r], NEG) for r in range(A_REP)]
        shift = [-slopes[r] * delta for r in range(A_REP)]
        return _flash_step(q3b, skb[pl.ds(off, tk), :], svb[pl.ds(off, tk), :], bias, shift, *carry)

    carry = lax.fori_loop(0, j_diag, lambda j, c: slc_step(j, c, False), (m0, l0, a0))
    _, l_s, acc_s = slc_step(j_diag, carry, True)
    o_slc = acc_s / l_s

    def win_body(j, carry):
        off = pl.multiple_of(j * tk, tk)
        delta = (s0 - j * tk).astype(F32)
        dist = rc + delta
        ok = (dist >= 0.0) & (dist < float(WINDOW))
        bias = [jnp.where(ok, bias0[r], NEG) for r in range(A_REP)]
        shift = [-slopes[r] * delta for r in range(A_REP)]
        return _flash_step(q3b, wkb[pl.ds(off, tk), :], wvb[pl.ds(off, tk), :], bias, shift, *carry)

    j_lo = jnp.maximum(s0 - (WINDOW - 1), 0) // tk
    _, l_w, acc_w = lax.fori_loop(j_lo, j_diag + 1, win_body, (m0, l0, a0))
    o_win = acc_w / l_w

    gt = jax.nn.sigmoid(gt_ref[...])
    for r in range(A_REP):
        c = []
        for n in range(N_GATES):
            i0 = (0 * A_REP + r) * N_GATES + n
            i1 = (1 * A_REP + r) * N_GATES + n
            c.append(jnp.where(g == 0, gt[:, i0:i0 + 1], gt[:, i1:i1 + 1]))
        sl = slice(r * tq, (r + 1) * tq)
        out = c[0] * o_cmp[r] + c[1] * o_slc[sl] + c[2] * o_win[sl]
        o_ref[:, r * HEAD_DIM:(r + 1) * HEAD_DIM] = out.astype(BF16)


def _nsa_prompt(proj, cmp_wk, cmp_wv, e_mat, layer, n_b, t_len):
    tq = 128
    tk = _pick_tile(t_len, PROMPT_TK)
    nq = t_len // tq
    nb = t_len // NSA_BLOCK
    gw = A_REP * HEAD_DIM
    seq = lambda cb: pl.BlockSpec((t_len, HEAD_DIM), lambda b, g, i, cb=cb: (b, cb + g))
    wp = pl.BlockSpec((None, NSA_BLOCK, HEAD_DIM), lambda b, g, i: (layer, 0, 0))
    return pl.pallas_call(
        functools.partial(_nsa_kernel, t_len=t_len, tq=tq, tk=tk), grid=(n_b, A_KV, nq),
        in_specs=[pl.BlockSpec((tq, gw), lambda b, g, i: (b * nq + i, g)),
                  seq(CB_NSA), seq(CB_NSA + 2), seq(CB_NSA + 4), seq(CB_NSA + 6), seq(CB_WIN), seq(CB_WIN + 2),
                  pl.BlockSpec((tq, LANE), lambda b, g, i: (b * nq + i, CB_GATE)),
                  wp, wp, pl.BlockSpec((nb, t_len), lambda b, g, i: (0, 0))],
        out_specs=pl.BlockSpec((tq, gw), lambda b, g, i: (b * nq + i, g)),
        out_shape=jax.ShapeDtypeStruct((n_b * t_len, A_HEADS * HEAD_DIM), BF16),
        scratch_shapes=[pltpu.VMEM((nb, HEAD_DIM), BF16), pltpu.VMEM((nb, HEAD_DIM), BF16),
                        pltpu.VMEM((t_len, HEAD_DIM), BF16), pltpu.VMEM((t_len, HEAD_DIM), BF16),
                        pltpu.VMEM((t_len, HEAD_DIM), BF16), pltpu.VMEM((t_len, HEAD_DIM), BF16),
                        pltpu.VMEM((t_len // tk, tq, tk), F32)],
        compiler_params=_cparams(("arbitrary", "arbitrary", "arbitrary"), 56), name="nsa_prompt")(
            proj, proj, proj, proj, proj, proj, proj, proj, cmp_wk, cmp_wv, e_mat)


DIFF_HEADS_PER_STEP = 2


def _diff_kernel(lam_ref, q_ref, k_ref, v_ref, g_ref, o_ref, kb, vb, *, tq, tk, lam_init):
    hp = pl.program_id(1)
    i = pl.program_id(2)
    nh = DIFF_HEADS_PER_STEP

    @pl.when(i == 0)
    def _():
        kb[...] = k_ref[...].astype(BF16)
        vb[...] = v_ref[...].astype(BF16)

    s0 = i * tq
    lane = lax.broadcasted_iota(I32, (tq, HEAD_DIM), 1)
    rc = (lax.broadcasted_iota(I32, (tq, tk), 0) - lax.broadcasted_iota(I32, (tq, tk), 1)).astype(F32)
    j_diag = s0 // tk
    slopes, qs, bias0 = [], [], []
    for u in range(nh):
        cols = slice(u * HEAD_DIM, (u + 1) * HEAD_DIM)
        slope = _select_by_index(hp, [float(SL_B[nh * j + u]) for j in range(B_HEADS // nh)])
        q = q_ref[:, cols] * (B_HALF ** -0.5)
        qs.append(jnp.concatenate([jnp.where(lane < B_HALF, q, 0.0), jnp.where(lane >= B_HALF, q, 0.0)],
                                  axis=0).astype(BF16))
        slopes.append(slope)
        bias0.append(-slope * rc)

    def step(j, carries, causal):
        off = pl.multiple_of(j * tk, tk)
        delta = (s0 - j * tk).astype(F32)
        out = []
        for u in range(nh):
            cols = slice(u * HEAD_DIM, (u + 1) * HEAD_DIM)
            bias = jnp.where(rc + delta >= 0.0, bias0[u], NEG) if causal else bias0[u]
            shift = -slopes[u] * delta
            out.append(_flash_step(qs[u], kb[pl.ds(off, tk), cols], vb[pl.ds(off, tk), cols], [bias, bias],
                                   [shift, shift], *carries[u]))
        return tuple(out)

    m0 = jnp.full((2 * tq, 1), NEG, F32)
    l0 = jnp.zeros((2 * tq, 1), F32)
    a0 = jnp.zeros((2 * tq, HEAD_DIM), F32)
    carries = lax.fori_loop(0, j_diag, lambda j, c: step(j, c, False), tuple((m0, l0, a0) for _ in range(nh)))
    carries = step(j_diag, carries, True)
    for u in range(nh):
        _, l, acc = carries[u]
        o = acc / l
        o = o[:tq] - lam_ref[0] * o[tq:]
        o = o * lax.rsqrt(jnp.mean(o * o, axis=-1, keepdims=True) + RMS_EPS) * g_ref[...] * (1.0 - lam_init)
        o_ref[:, u * HEAD_DIM:(u + 1) * HEAD_DIM] = o.astype(BF16)


def _diff_prompt(lam, proj, diff_g, layer, n_b, t_len, lam_init):
    tq = 128
    tk = _pick_tile(t_len, PROMPT_TK)
    nq = t_len // tq
    nh = DIFF_HEADS_PER_STEP
    gw = nh * HEAD_DIM
    assert B_HEADS % nh == 0 and CB_BQ % nh == 0 and CB_BK % nh == 0 and CB_BV % nh == 0
    seq = lambda cb: pl.BlockSpec((t_len, gw), lambda b, h, i, cb=cb: (b, cb // nh + h))
    return pl.pallas_call(
        functools.partial(_diff_kernel, tq=tq, tk=tk, lam_init=lam_init), grid=(n_b, B_HEADS // nh, nq),
        in_specs=[pl.BlockSpec(memory_space=pltpu.SMEM),
                  pl.BlockSpec((tq, gw), lambda b, h, i: (b * nq + i, CB_BQ // nh + h)),
                  seq(CB_BK), seq(CB_BV),
                  pl.BlockSpec((None, 1, HEAD_DIM), lambda b, h, i: (layer, 0, 0))],
        out_specs=pl.BlockSpec((tq, gw), lambda b, h, i: (b * nq + i, h)),
        out_shape=jax.ShapeDtypeStruct((n_b * t_len, B_HEADS * HEAD_DIM), BF16),
        scratch_shapes=[pltpu.VMEM((t_len, gw), BF16), pltpu.VMEM((t_len, gw), BF16)],
        compiler_params=_cparams(("arbitrary", "arbitrary", "arbitrary"), 40), name="diff_prompt")(
            lam, proj, proj, proj, diff_g)


def _sgu_kernel(u_ref, v_ref, g_ref, b_ref, w_ref, bt_ref, o_ref, *, n_chunks):
    u = jax.nn.gelu(u_ref[...])
    v = _ln_rows(jax.nn.gelu(v_ref[...]), g_ref[...], b_ref[...])
    tri = lax.broadcasted_iota(I32, (CHUNK, CHUNK), 0) >= lax.broadcasted_iota(I32, (CHUNK, CHUNK), 1)
    for gi in range(C_GROUPS):
        w = jnp.where(tri, w_ref[gi], 0.0).astype(BF16)
        cols = slice(gi * HEAD_DIM, (gi + 1) * HEAD_DIM)
        for c in range(n_chunks):
            rows = slice(c * CHUNK, (c + 1) * CHUNK)
            s = jnp.dot(w, v[rows, cols].astype(BF16), preferred_element_type=F32) + bt_ref[:, gi:gi + 1]
            o_ref[rows, cols] = (u[rows, cols] * s).astype(BF16)


def _sgu_prompt(proj, ln_g, ln_b, sgu_w, sgu_bt, layer, m_p):
    n_chunks = _pick_tile(m_p // CHUNK, (4, 2, 1))
    tr = n_chunks * CHUNK
    vec = pl.BlockSpec((None, 1, C_WIDTH), lambda i: (layer, 0, 0))
    return pl.pallas_call(
        functools.partial(_sgu_kernel, n_chunks=n_chunks), grid=(m_p // tr,),
        in_specs=[pl.BlockSpec((tr, C_WIDTH), lambda i: (i, CB_CU // C_GROUPS)),
                  pl.BlockSpec((tr, C_WIDTH), lambda i: (i, CB_CV // C_GROUPS)),
                  vec, vec,
                  pl.BlockSpec((None, C_GROUPS, CHUNK, CHUNK), lambda i: (layer, 0, 0, 0)),
                  pl.BlockSpec((None, CHUNK, C_GROUPS), lambda i: (layer, 0, 0))],
        out_specs=pl.BlockSpec((tr, C_WIDTH), lambda i: (i, 0)),
        out_shape=jax.ShapeDtypeStruct((m_p, C_WIDTH), BF16),
        compiler_params=_cparams(("arbitrary",)), name="sgu_prompt")(proj, proj, ln_g, ln_b, sgu_w, sgu_bt)


NSA_SLABS = 4 * A_KV
DIFF_SLABS = 2 * B_HEADS
WIN_SLABS = 2 * A_KV


def _page_specs(n_pages_step, rows, layer):
    return [pl.BlockSpec((None, None, rows, HEAD_DIM),
                         lambda b, c, pt, i=i: (layer, pt[b, c * n_pages_step + i], 0, 0))
            for i in range(n_pages_step)]


def _slab(ref, slab, n_slabs):
    return ref[pl.ds(slab, ref.shape[0] // n_slabs, stride=n_slabs), :]


def _pool_kernel(pt_ref, *refs, pg):
    del pt_ref
    pages, wk_ref, wv_ref, o_ref = refs[:pg], refs[pg], refs[pg + 1], refs[pg + 2]
    per_page = pages[0].shape[0] // NSA_SLABS // NSA_BLOCK
    for i in range(pg):
        for slab in range(2 * A_KV):
            x = _slab(pages[i], slab, NSA_SLABS)
            w = wk_ref[...] if slab < A_KV else wv_ref[...]
            for hb in range(per_page):
                r = i * per_page + hb
                o_ref[r:r + 1, slab * HEAD_DIM:(slab + 1) * HEAD_DIM] = jnp.sum(
                    x[hb * NSA_BLOCK:(hb + 1) * NSA_BLOCK] * w, axis=0, keepdims=True)


def _sample_pool(page_table, cache_nsa4, cmp_wk, cmp_wv, layer, pg):
    n_s, n_pages = page_table.shape
    rows = cache_nsa4.shape[2]
    half = 2 * A_KV * HEAD_DIM
    per_page = rows // NSA_SLABS // NSA_BLOCK
    nc = n_pages // pg
    wp = pl.BlockSpec((None, NSA_BLOCK, HEAD_DIM), lambda b, c, pt: (layer, 0, 0))
    return pl.pallas_call(
        functools.partial(_pool_kernel, pg=pg),
        grid_spec=pltpu.PrefetchScalarGridSpec(
            num_scalar_prefetch=1, grid=(n_s, nc),
            in_specs=_page_specs(pg, rows, layer) + [wp, wp],
            out_specs=pl.BlockSpec((None, pg * per_page, half), lambda b, c, pt: (b, c, 0))),
        out_shape=jax.ShapeDtypeStruct((n_s, n_pages * per_page, half), F32),
        compiler_params=_cparams(("arbitrary", "arbitrary"), 32), name="sample_cmp_pool")(
            page_table, *([cache_nsa4] * pg), cmp_wk, cmp_wv)


def _q3_rows(prow, g):
    rowi = lax.broadcasted_iota(I32, (8, HEAD_DIM), 0)
    q3 = jnp.zeros((8, HEAD_DIM), F32)
    for r in range(A_REP):
        c0 = (CB_AQ + g * A_REP + r) * HEAD_DIM
        q3 = jnp.where(rowi == r, jnp.broadcast_to(prow[:, c0:c0 + HEAD_DIM], (8, HEAD_DIM)), q3)
    return q3 * (HEAD_DIM ** -0.5)


def _select_kernel(p_ref, kv_ref, sl_ref, e_ref, ocmp_ref, km_ref, *, p_len, nbp, bpc):
    prow = p_ref[...]
    n_blocks = nbp + 1
    n_sel = min(N_SELECT, n_blocks)
    blk = lax.broadcasted_iota(I32, (8, nbp), 1)
    dist = (p_len - (blk * NSA_BLOCK + NSA_BLOCK - 1)).astype(F32)
    lane_f = lax.broadcasted_iota(I32, (1, nbp), 1).astype(F32)
    rowi = lax.broadcasted_iota(I32, (8, nbp), 0)
    sel_rows = jnp.zeros((8, nbp), F32)
    outs = []
    for g in range(A_KV):
        q3 = _q3_rows(prow, g).astype(BF16)
        kc = kv_ref[:, g * HEAD_DIM:(g + 1) * HEAD_DIM].astype(BF16)
        vc = kv_ref[:, (A_KV + g) * HEAD_DIM:(A_KV + g + 1) * HEAD_DIM].astype(BF16)
        s = lax.dot_general(q3, kc, NT_DIMS, preferred_element_type=F32) - sl_ref[g] * dist
        m = jnp.max(s, axis=-1, keepdims=True)
        e = jnp.exp(s - m)
        p = e / jnp.maximum(jnp.sum(e, axis=-1, keepdims=True), 1e-30)
        o = jnp.dot(p.astype(BF16), vc, preferred_element_type=F32)
        outs += [o[r:r + 1] for r in range(A_REP)]
        work = p[0:1] + p[1:2] + p[2:3]
        sel = jnp.zeros((1, nbp), F32)
        for _ in range(n_sel - 1):
            mx = jnp.max(work, axis=-1, keepdims=True)
            idx = jnp.min(jnp.where(work == mx, lane_f, float(nbp)), axis=-1, keepdims=True)
            hit = lane_f == idx
            sel = jnp.where(hit, 1.0, sel)
            work = jnp.where(hit, -jnp.inf, work)
        sel_rows = jnp.where(rowi == g, sel, sel_rows)
    ocmp_ref[...] = jnp.concatenate(outs, axis=1)
    kpc = bpc * NSA_BLOCK
    for c in range(nbp // bpc):
        in_chunk = (blk >= c * bpc) & (blk < (c + 1) * bpc)
        km_ref[:, c * kpc:(c + 1) * kpc] = jnp.dot(jnp.where(in_chunk, sel_rows, 0.0).astype(BF16), e_ref[...],
                                                   preferred_element_type=F32)


def _sample_select(prow, kvcmp, sl_cols, e_small, p_len, bpc):
    n_s = prow.shape[0]
    nbp = kvcmp.shape[1]
    half = kvcmp.shape[2]
    return pl.pallas_call(
        functools.partial(_select_kernel, p_len=p_len, nbp=nbp, bpc=bpc), grid=(n_s,),
        in_specs=[pl.BlockSpec((None, 1, N_PROJ), lambda b: (b, 0, 0)),
                  pl.BlockSpec((None, nbp, half), lambda b: (b, 0, 0)),
                  pl.BlockSpec((A_KV, 8, 1), lambda b: (0, 0, 0)),
                  pl.BlockSpec((nbp, bpc * NSA_BLOCK), lambda b: (0, 0))],
        out_specs=[pl.BlockSpec((None, 1, A_HEADS * HEAD_DIM), lambda b: (b, 0, 0)),
                   pl.BlockSpec((None, 8, p_len), lambda b: (b, 0, 0))],
        out_shape=[jax.ShapeDtypeStruct((n_s, 1, A_HEADS * HEAD_DIM), F32),
                   jax.ShapeDtypeStruct((n_s, 8, p_len), F32)],
        compiler_params=_cparams(("arbitrary",), 32), name="sample_select")(prow, kvcmp, sl_cols, e_small)


def _decode_update(s, ok, vb_list, m_ref, l_ref, acc_ref, page):
    sm = s if ok is None else jnp.where(ok, s, NEG)
    m_old = m_ref[...]
    m_new = jnp.maximum(m_old, jnp.max(sm, axis=-1, keepdims=True))
    a = jnp.exp(m_old - m_new)
    p = jnp.exp(sm - m_new)
    if ok is not None:
        p = jnp.where(ok, p, 0.0)
    l_ref[...] = a * l_ref[...] + jnp.sum(p, axis=-1, keepdims=True)
    pb = p.astype(BF16)
    pv = jnp.dot(pb[:, 0:page], vb_list[0], preferred_element_type=F32)
    for i in range(1, len(vb_list)):
        pv = pv + jnp.dot(pb[:, i * page:(i + 1) * page], vb_list[i], preferred_element_type=F32)
    acc_ref[...] = a * acc_ref[...] + pv
    m_ref[...] = m_new


def _nsa_decode_kernel(pt_ref, *refs, pg, p_len, n_win):
    del pt_ref
    pages = refs[:pg]
    (p_ref, km_ref, win_ref, ocmp_ref, sl_ref, sg_ref, sb_ref, w00_ref, b0_ref,
     o_ref, osgu_ref, vrow_ref, m_s, l_s, acc_s) = refs[pg:]
    c = pl.program_id(1)
    page = pages[0].shape[0] // NSA_SLABS
    gw = A_KV * HEAD_DIM

    @pl.when(c == 0)
    def _():
        m_s[...] = jnp.full_like(m_s, NEG)
        l_s[...] = jnp.zeros_like(l_s)
        acc_s[...] = jnp.zeros_like(acc_s)

    prow = p_ref[...]
    kpos = c * (pg * page) + lax.broadcasted_iota(I32, (8, pg * page), 1)
    dist = (p_len - kpos).astype(F32)
    for g in range(A_KV):
        q3 = _q3_rows(prow, g).astype(BF16)
        s = jnp.concatenate(
            [lax.dot_general(q3, _slab(pages[i], 2 * A_KV + g, NSA_SLABS).astype(BF16), NT_DIMS,
                             preferred_element_type=F32) for i in range(pg)], axis=1)
        s = s - sl_ref[g] * dist
        ok = jnp.broadcast_to(km_ref[g:g + 1, :], s.shape) > 0.5
        vbs = [_slab(pages[i], 3 * A_KV + g, NSA_SLABS).astype(BF16) for i in range(pg)]
        _decode_update(s, ok, vbs, m_s.at[g], l_s.at[g], acc_s.at[g], page)

    @pl.when(c == pl.num_programs(1) - 1)
    def _():
        gt = jax.nn.sigmoid(prow[:, CB_GATE * LANE:(CB_GATE + 1) * LANE])
        ocmp = ocmp_ref[...]
        wpos = p_len - n_win + lax.broadcasted_iota(I32, (8, n_win), 1)
        wd = (p_len - wpos).astype(F32)
        wok = (wd >= 0.0) & (wd < float(WINDOW))
        outs = []
        for g in range(A_KV):
            q3 = _q3_rows(prow, g)
            q3b = q3.astype(BF16)
            nsa0 = (CB_NSA + 2 * A_KV) * HEAD_DIM
            k_new = prow[:, nsa0 + g * HEAD_DIM:nsa0 + (g + 1) * HEAD_DIM]
            v_new = prow[:, nsa0 + gw + g * HEAD_DIM:nsa0 + gw + (g + 1) * HEAD_DIM]
            s_new = jnp.sum(q3b.astype(F32) * k_new.astype(BF16).astype(F32), axis=-1, keepdims=True)
            m_old = m_s[g]
            m_f = jnp.maximum(m_old, s_new)
            a = jnp.exp(m_old - m_f)
            pn = jnp.exp(s_new - m_f)
            o_slc = (a * acc_s[g] + pn * v_new.astype(BF16).astype(F32)) / (a * l_s[g] + pn)
            wk = _slab(win_ref, g, WIN_SLABS).astype(BF16)
            wv = _slab(win_ref, A_KV + g, WIN_SLABS).astype(BF16)
            win0 = CB_WIN * HEAD_DIM
            wk_new = prow[:, win0 + g * HEAD_DIM:win0 + (g + 1) * HEAD_DIM]
            wv_new = prow[:, win0 + gw + g * HEAD_DIM:win0 + gw + (g + 1) * HEAD_DIM]
            sw = lax.dot_general(q3b, wk, NT_DIMS, preferred_element_type=F32) - sl_ref[g] * wd
            sw = jnp.where(wok, sw, NEG)
            sw_new = jnp.sum(q3b.astype(F32) * wk_new.astype(BF16).astype(F32), axis=-1, keepdims=True)
            mw = jnp.maximum(jnp.max(sw, axis=-1, keepdims=True), sw_new)
            pw = jnp.where(wok, jnp.exp(sw - mw), 0.0)
            pwn = jnp.exp(sw_new - mw)
            o_win = ((jnp.dot(pw.astype(BF16), wv, preferred_element_type=F32) + pwn * wv_new.astype(BF16).astype(F32))
                     / (jnp.sum(pw, axis=-1, keepdims=True) + pwn))
            for r in range(A_REP):
                hd = g * A_REP + r
                gi = hd * N_GATES
                oc = ocmp[:, hd * HEAD_DIM:(hd + 1) * HEAD_DIM]
                outs.append(gt[:, gi:gi + 1] * oc + gt[:, gi + 1:gi + 2] * o_slc[r:r + 1]
                            + gt[:, gi + 2:gi + 3] * o_win[r:r + 1])
        o_ref[...] = jnp.concatenate(outs, axis=1)
        u = jax.nn.gelu(prow[:, CB_CU * LANE:CB_CU * LANE + C_WIDTH])
        v = _ln_rows(jax.nn.gelu(prow[:, CB_CV * LANE:CB_CV * LANE + C_WIDTH]), sg_ref[...], sb_ref[...])
        vrow_ref[...] = v
        osgu_ref[...] = u * (w00_ref[...] * v + b0_ref[...])


def _sample_nsa(page_table, cache_nsa4, prow, km, win_l, ocmp, sl_cols, sgu_g, sgu_b, w00, b0, layer, pg, p_len):
    n_s, n_pages = page_table.shape
    rows = cache_nsa4.shape[2]
    page = rows // NSA_SLABS
    n_win = win_l.shape[2] // WIN_SLABS
    nc = n_pages // pg
    one = lambda w: pl.BlockSpec((None, 1, w), lambda b, c, pt: (b, 0, 0))
    lvec = pl.BlockSpec((None, 1, C_WIDTH), lambda b, c, pt: (layer, 0, 0))
    return pl.pallas_call(
        functools.partial(_nsa_decode_kernel, pg=pg, p_len=p_len, n_win=n_win),
        grid_spec=pltpu.PrefetchScalarGridSpec(
            num_scalar_prefetch=1, grid=(n_s, nc),
            in_specs=_page_specs(pg, rows, layer) + [
                one(N_PROJ),
                pl.BlockSpec((None, 8, pg * page), lambda b, c, pt: (b, 0, c)),
                pl.BlockSpec((None, None, n_win * WIN_SLABS, HEAD_DIM), lambda b, c, pt: (layer, b, 0, 0)),
                one(A_HEADS * HEAD_DIM),
                pl.BlockSpec((A_KV, 8, 1), lambda b, c, pt: (0, 0, 0)),
                lvec, lvec, lvec, lvec],
            out_specs=[one(A_HEADS * HEAD_DIM), one(C_WIDTH), one(C_WIDTH)],
            scratch_shapes=[pltpu.VMEM((A_KV, 8, 1), F32), pltpu.VMEM((A_KV, 8, 1), F32),
                            pltpu.VMEM((A_KV, 8, HEAD_DIM), F32)]),
        out_shape=[jax.ShapeDtypeStruct((n_s, 1, A_HEADS * HEAD_DIM), F32),
                   jax.ShapeDtypeStruct((n_s, 1, C_WIDTH), F32), jax.ShapeDtypeStruct((n_s, 1, C_WIDTH), F32)],
        compiler_params=_cparams(("arbitrary", "arbitrary"), 40), name="sample_nsa")(
            page_table, *([cache_nsa4] * pg), prow, km, win_l, ocmp, sl_cols, sgu_g, sgu_b, w00, b0)


def _diff_decode_kernel(pt_ref, *refs, pg, p_len, lam_init):
    del pt_ref
    pages = refs[:pg]
    p_ref, g_ref, lam_ref, o_ref, m_s, l_s, acc_s = refs[pg:]
    c = pl.program_id(1)
    page = pages[0].shape[0] // DIFF_SLABS

    @pl.when(c == 0)
    def _():
        m_s[...] = jnp.full_like(m_s, NEG)
        l_s[...] = jnp.zeros_like(l_s)
        acc_s[...] = jnp.zeros_like(acc_s)

    prow = p_ref[...]
    rowi = lax.broadcasted_iota(I32, (8, HEAD_DIM), 0)
    half = lax.shift_right_logical(lax.broadcasted_iota(I32, (8, HEAD_DIM), 1), int(math.log2(B_HALF)))
    kpos = c * (pg * page) + lax.broadcasted_iota(I32, (8, pg * page), 1)
    dist = (p_len - kpos).astype(F32)

    def q_rows(h):
        c0 = (CB_BQ + h) * LANE
        q = jnp.broadcast_to(prow[:, c0:c0 + HEAD_DIM], (8, HEAD_DIM)) * (B_HALF ** -0.5)
        return jnp.where(half == rowi, q, 0.0).astype(BF16)

    for h in range(B_HEADS):
        qh = q_rows(h)
        s = jnp.concatenate([lax.dot_general(qh, _slab(pages[i], 2 * h, DIFF_SLABS).astype(BF16), NT_DIMS,
                                             preferred_element_type=F32) for i in range(pg)], axis=1)
        s = s - float(SL_B[h]) * dist
        vbs = [_slab(pages[i], 2 * h + 1, DIFF_SLABS).astype(BF16) for i in range(pg)]
        _decode_update(s, None, vbs, m_s.at[h], l_s.at[h], acc_s.at[h], page)

    @pl.when(c == pl.num_programs(1) - 1)
    def _():
        outs = []
        for h in range(B_HEADS):
            qh = q_rows(h).astype(F32)
            k_new = prow[:, (CB_BK + h) * LANE:(CB_BK + h + 1) * LANE].astype(BF16).astype(F32)
            v_new = prow[:, (CB_BV + h) * LANE:(CB_BV + h + 1) * LANE].astype(BF16).astype(F32)
            s_new = jnp.sum(qh * k_new, axis=-1, keepdims=True)
            m_old = m_s[h]
            m_f = jnp.maximum(m_old, s_new)
            a = jnp.exp(m_old - m_f)
            pn = jnp.exp(s_new - m_f)
            o = (a * acc_s[h] + pn * v_new) / (a * l_s[h] + pn)
            oh = o[0:1] - lam_ref[0] * o[1:2]
            oh = oh * lax.rsqrt(jnp.mean(oh * oh, axis=-1, keepdims=True) + RMS_EPS) * g_ref[...] * (1.0 - lam_init)
            outs.append(oh)
        o_ref[...] = jnp.concatenate(outs, axis=1)


def _sample_diff(page_table, lam, cache_diff4, prow, diff_g, layer, pg, p_len, lam_init):
    n_s, n_pages = page_table.shape
    rows = cache_diff4.shape[2]
    hw = B_HEADS * HEAD_DIM
    nc = n_pages // pg
    return pl.pallas_call(
        functools.partial(_diff_decode_kernel, pg=pg, p_len=p_len, lam_init=lam_init),
        grid_spec=pltpu.PrefetchScalarGridSpec(
            num_scalar_prefetch=1, grid=(n_s, nc),
            in_specs=_page_specs(pg, rows, layer) + [
                pl.BlockSpec((None, 1, N_PROJ), lambda b, c, pt: (b, 0, 0)),
                pl.BlockSpec((None, 1, HEAD_DIM), lambda b, c, pt: (layer, 0, 0)),
                pl.BlockSpec(memory_space=pltpu.SMEM)],
            out_specs=pl.BlockSpec((None, 1, hw), lambda b, c, pt: (b, 0, 0)),
            scratch_shapes=[pltpu.VMEM((B_HEADS, 8, 1), F32), pltpu.VMEM((B_HEADS, 8, 1), F32),
                            pltpu.VMEM((B_HEADS, 8, HEAD_DIM), F32)]),
        out_shape=jax.ShapeDtypeStruct((n_s, 1, hw), F32),
        compiler_params=_cparams(("arbitrary", "arbitrary"), 40), name="sample_diff")(
            page_table, *([cache_diff4] * pg), prow, diff_g, lam)


def kernel(x_prompt, x_sample, cache_nsa, cache_diff, state_win, page_table, ln_in_g, ln_in_b, w_in, cmp_wk, cmp_wv,
           lam_q1, lam_k1, lam_q2, lam_k2, diff_norm_g, sgu_ln_g, sgu_ln_b, sgu_w, sgu_b, w_out, ln1_g, ln1_b,
           router_w, router_b, w_up, b_up, w_down, b_down, ln2_g, ln2_b):
    n_b, t_len, d = x_prompt.shape
    n_s, t_s, _ = x_sample.shape
    depth = w_in.shape[0]
    assert t_s == 1 and t_len % CHUNK == 0
    n_pages = page_table.shape[1]
    page = cache_nsa.shape[2]
    p_len = n_pages * page
    n_win = state_win.shape[2]
    assert p_len % CHUNK == 0 and page % NSA_BLOCK == 0 and n_win == min(WINDOW, p_len)
    m_p = n_b * t_len
    n_valid = m_p + n_s
    mt = -(-n_valid // 256) * 256
    alpha = (2 * depth) ** 0.25
    pg = _pick_tile(n_pages, (8, 4, 2, 1))
    bpc = pg * (page // NSA_BLOCK)

    w_in_r = jnp.concatenate(
        [w_in[:, :, :GATE_ORIG], w_in[:, :, GATE_ORIG + N_GATE_COLS:], w_in[:, :, GATE_ORIG:GATE_ORIG + N_GATE_COLS],
         jnp.zeros((depth, d, LANE - N_GATE_COLS), w_in.dtype)], axis=-1).astype(BF16)
    assert w_in_r.shape[-1] == N_PROJ
    w_out_b = w_out.astype(BF16)
    p_even = jnp.asarray(np.arange(2 * LANE)[:, None] == 2 * np.arange(LANE)[None, :], BF16)
    mw = {"w_up": w_up, "b_up": b_up[:, :, None, :], "w_down": w_down, "b_down": b_down[:, :, None, :],
          "p_even": p_even}
    rw_pad = jnp.pad(router_w, ((0, 0), (0, 0), (0, LANE - N_EXPERTS)))
    rb_pad = jnp.pad(router_b, ((0, 0), (0, LANE - N_EXPERTS)), constant_values=NEG)[:, None, :]
    ln1_g3, ln1_b3, ln2_g3, ln2_b3 = (a[:, None, :] for a in (ln1_g, ln1_b, ln2_g, ln2_b))
    sgu_g3, sgu_b3 = sgu_ln_g[:, None, :], sgu_ln_b[:, None, :]
    sgu_bt = jnp.swapaxes(sgu_b, 1, 2)
    sgu_w00 = jnp.repeat(sgu_w[:, :, 0, 0], HEAD_DIM, axis=1)[:, None, :]
    sgu_b0 = jnp.repeat(sgu_b[:, :, 0], HEAD_DIM, axis=1)[:, None, :]
    diff_g3 = diff_norm_g[:, None, :]
    nb_p = t_len // NSA_BLOCK
    e_prompt = jnp.asarray((np.arange(t_len)[None, :] // NSA_BLOCK == np.arange(nb_p)[:, None]), BF16)
    e_small = jnp.asarray((np.arange(bpc * NSA_BLOCK)[None, :] // NSA_BLOCK == np.arange(p_len // NSA_BLOCK)[:, None] % bpc), BF16)
    sl_a_cols = jnp.asarray(np.concatenate([SL_A, np.zeros((A_KV, 8 - A_REP), np.float32)], axis=1)[:, :, None])
    cache_nsa4 = cache_nsa.reshape(depth, cache_nsa.shape[1], page * NSA_SLABS, HEAD_DIM)
    cache_diff4 = jnp.swapaxes(cache_diff, 3, 4).reshape(depth, cache_diff.shape[1], page * DIFF_SLABS, HEAD_DIM)
    win4 = state_win.reshape(depth, n_s, n_win * WIN_SLABS, HEAD_DIM)

    x_all = jnp.concatenate([x_prompt.reshape(m_p, d), x_sample.reshape(n_s, d),
                             jnp.zeros((mt - n_valid, d), x_prompt.dtype)], axis=0)
    xf, xb = _ln_in(x_all, ln_in_g, ln_in_b)

    nsa_p, nsa_s, diff_p, diff_s, win_p, win_s, sgu_s = [], [], [], [], [], [], []
    for l in range(depth):
        lam_init = 0.8 - 0.6 * math.exp(-0.3 * l)
        lam = (jnp.exp(jnp.sum(lam_q1[l] * lam_k1[l])) - jnp.exp(jnp.sum(lam_q2[l] * lam_k2[l])) + lam_init)
        lam = lam.astype(F32).reshape(1)
        proj = _proj(xb, w_in_r, l)
        prow = proj[m_p:n_valid].reshape(n_s, 1, N_PROJ)

        o_nsa = _nsa_prompt(proj, cmp_wk, cmp_wv, e_prompt, l, n_b, t_len)
        o_diff = _diff_prompt(lam, proj, diff_g3, l, n_b, t_len, lam_init)
        o_sgu = _sgu_prompt(proj, sgu_g3, sgu_b3, sgu_w, sgu_bt, l, m_p)

        kvcmp = _sample_pool(page_table, cache_nsa4, cmp_wk, cmp_wv, l, pg)
        ocmp_s, km = _sample_select(prow, kvcmp, sl_a_cols, e_small, p_len, bpc)
        o_nsa_s, o_sgu_s, v_rows = _sample_nsa(page_table, cache_nsa4, prow, km, win4, ocmp_s, sl_a_cols,
                                               sgu_g3, sgu_b3, sgu_w00, sgu_b0, l, pg, p_len)
        o_diff_s = _sample_diff(page_table, lam, cache_diff4, prow, diff_g3, l, pg, p_len, lam_init)

        pad = lambda w: jnp.zeros((mt - n_valid, w), BF16)
        hn = jnp.concatenate([o_nsa, o_nsa_s.reshape(n_s, -1).astype(BF16), pad(o_nsa.shape[1])], axis=0)
        hd = jnp.concatenate([o_diff, o_diff_s.reshape(n_s, -1).astype(BF16), pad(o_diff.shape[1])], axis=0)
        hs = jnp.concatenate([o_sgu, o_sgu_s.reshape(n_s, -1).astype(BF16), pad(o_sgu.shape[1])], axis=0)
        x1, x1_packed, logits = _outproj(hn, hd, hs, w_out_b, xf, ln1_g3, ln1_b3, rw_pad, rb_pad, l, alpha)
        xf, xb = _moe_layer(x1, x1_packed, logits, n_valid, mw, ln2_g3, ln2_b3, l, alpha)

        c0, c1 = CB_NSA * LANE, CB_WIN * LANE
        nsa_p.append(proj[:m_p, c0:c1].reshape(n_b, t_len, 4, A_KV, HEAD_DIM))
        nsa_s.append(proj[m_p:n_valid, c0:c1].reshape(n_s, 1, 4, A_KV, HEAD_DIM))
        d0, d1 = CB_BK * LANE, CB_CU * LANE
        diff_p.append(proj[:m_p, d0:d1].reshape(n_b, t_len, 2, B_HEADS, HEAD_DIM))
        diff_s.append(proj[m_p:n_valid, d0:d1].reshape(n_s, 1, 2, B_HEADS, HEAD_DIM))
        w_rows = proj[:m_p, c1:CB_BQ * LANE].reshape(n_b, t_len, 2, A_KV, HEAD_DIM)
        win_p.append(w_rows[:, t_len - min(WINDOW, t_len):])
        w_new = proj[m_p:n_valid, c1:CB_BQ * LANE].reshape(n_s, 1, 2, A_KV, HEAD_DIM)
        win_s.append(jnp.concatenate([state_win[l], w_new], axis=1)[:, 1:])
        sgu_s.append(v_rows.reshape(n_s, 1, C_WIDTH))

    y_p = xf[:m_p].reshape(n_b, t_len, d)
    y_s = xf[m_p:n_valid].reshape(n_s, 1, d)
    return (y_p, y_s, jnp.stack(nsa_p), jnp.stack(nsa_s), jnp.stack(diff_p), jnp.stack(diff_s),
            jnp.stack(win_p), jnp.stack(win_s), jnp.stack(sgu_s))
```

```python
import functools
import math

import numpy as np
import jax
import jax.numpy as jnp
from jax import lax
from jax.experimental import pallas as pl
from jax.experimental.pallas import tpu as pltpu

F32 = jnp.float32
BF16 = jnp.bfloat16
I32 = jnp.int32

HEAD_DIM = 128
A_KV = 2
A_REP = 3
A_HEADS = A_KV * A_REP
B_HEADS = 6
B_HALF = HEAD_DIM // 2
C_GROUPS = 4
C_WIDTH = C_GROUPS * HEAD_DIM
CHUNK = 128
NSA_BLOCK = 64
N_SELECT = 16
WINDOW = 512
N_GATES = 3
N_EXPERTS = 32
TOP_K = 4
SWIGLU_LIMIT = 7.0
SWIGLU_ALPHA = 1.702
LN_EPS = 1e-5
RMS_EPS = 1e-5
LANE = 128
NEG = -1e30

_SL = (2.0 ** (-8.0 * np.arange(1, A_HEADS + B_HEADS + 1) / (A_HEADS + B_HEADS))).astype(np.float32)
SL_A = _SL[0::2].reshape(A_KV, A_REP)
SL_B = _SL[1::2]

CB_AQ = 0
CB_NSA = 6
CB_WIN = 14
CB_BQ = 18
CB_BK = 24
CB_BV = 30
CB_CU = 36
CB_CV = 40
CB_GATE = 44
N_PROJ = 45 * LANE
N_GATE_COLS = A_HEADS * N_GATES
GATE_ORIG = 2304

NT_DIMS = (((1,), (1,)), ((), ()))
PROMPT_TK = (512, 256, 128)


def _cparams(sem, vmem_mb=None):
    kw = dict(dimension_semantics=sem)
    if vmem_mb is not None:
        kw["vmem_limit_bytes"] = vmem_mb * 1024 * 1024
    return pltpu.CompilerParams(**kw)


def _pick_tile(n, candidates):
    for c in candidates:
        if n % c == 0:
            return c
    return n


def _ln_rows(x, g, b):
    mu = jnp.mean(x, axis=-1, keepdims=True)
    xc = x - mu
    var = jnp.mean(xc * xc, axis=-1, keepdims=True)
    return xc * lax.rsqrt(var + LN_EPS) * g + b


def _pack_bf16_pairs(y):
    n = y.shape[1] // 2
    lo = pltpu.bitcast(y[:, :n].astype(BF16).astype(F32), jnp.uint32)
    hi = pltpu.bitcast(y[:, n:].astype(BF16).astype(F32), jnp.uint32)
    return (hi & jnp.uint32(0xFFFF0000)) | (lo >> 16)


def _unpack_bf16_pairs(w):
    lo = pltpu.bitcast(w << 16, F32)
    hi = pltpu.bitcast(w & jnp.uint32(0xFFFF0000), F32)
    return jnp.concatenate([lo, hi], axis=1).astype(BF16)


def _select_by_index(idx, values):
    out = values[-1]
    for j in range(len(values) - 2, -1, -1):
        out = jnp.where(idx == j, values[j], out)
    return out


def _ln_in_kernel(x_ref, g_ref, b_ref, of_ref, ob_ref):
    y = _ln_rows(x_ref[...], g_ref[...], b_ref[...])
    of_ref[...] = y
    ob_ref[...] = y.astype(BF16)


def _ln_in(x, g, b):
    mt, d = x.shape
    tm = _pick_tile(mt, (256, 128, 8))
    row = pl.BlockSpec((tm, d), lambda i: (i, 0))
    vec = pl.BlockSpec((1, d), lambda i: (0, 0))
    return pl.pallas_call(
        _ln_in_kernel, grid=(mt // tm,), in_specs=[row, vec, vec], out_specs=[row, row],
        out_shape=[jax.ShapeDtypeStruct((mt, d), F32), jax.ShapeDtypeStruct((mt, d), BF16)],
        compiler_params=_cparams(("arbitrary",)), name="ln_in")(x, g.reshape(1, d), b.reshape(1, d))


def _proj_kernel(x_ref, w_ref, o_ref):
    o_ref[...] = jnp.dot(x_ref[...], w_ref[...], preferred_element_type=F32)


def _proj(xb, w_all, layer):
    mt, d = xb.shape
    n = w_all.shape[-1]
    tm = _pick_tile(mt, (768, 512, 256, 128, 8))
    tn = _pick_tile(n, (1152, 640, 128))
    return pl.pallas_call(
        _proj_kernel, grid=(n // tn, mt // tm),
        in_specs=[pl.BlockSpec((tm, d), lambda j, i: (i, 0)),
                  pl.BlockSpec((None, d, tn), lambda j, i: (layer, 0, j))],
        out_specs=pl.BlockSpec((tm, tn), lambda j, i: (i, j)),
        out_shape=jax.ShapeDtypeStruct((mt, n), F32),
        compiler_params=_cparams(("arbitrary", "arbitrary"), 48), name="in_proj")(xb, w_all)


def _outproj_kernel(hn_ref, hd_ref, hs_ref, wn_ref, wd_ref, ws_ref, x_ref, g_ref, b_ref, rw_ref, rb_ref,
                    xf_ref, xp_ref, lg_ref, *, alpha):
    h = jnp.dot(hn_ref[...], wn_ref[...], preferred_element_type=F32)
    h = h + jnp.dot(hd_ref[...], wd_ref[...], preferred_element_type=F32)
    h = h + jnp.dot(hs_ref[...], ws_ref[...], preferred_element_type=F32)
    y = _ln_rows(alpha * x_ref[...] + h, g_ref[...], b_ref[...])
    xf_ref[...] = y
    xp_ref[...] = _pack_bf16_pairs(y)
    yh = y.astype(BF16)
    yl = (y - yh.astype(F32)).astype(BF16)
    rw = rw_ref[...]
    wh = rw.astype(BF16)
    wl = (rw - wh.astype(F32)).astype(BF16)
    lg = jnp.dot(yh, wh, preferred_element_type=F32)
    lg = lg + jnp.dot(yh, wl, preferred_element_type=F32)
    lg = lg + jnp.dot(yl, wh, preferred_element_type=F32)
    lg_ref[...] = lg + rb_ref[...]


def _outproj(hn, hd, hs, w_out_b, x, g, b, rw_pad, rb_pad, layer, alpha):
    mt, d = x.shape
    tm = _pick_tile(mt, (256, 128, 8))
    wn, wdw, wsw = hn.shape[1], hd.shape[1], hs.shape[1]
    assert wn == wdw and (wn + wdw) % wsw == 0
    row = lambda w: pl.BlockSpec((tm, w), lambda i: (i, 0))
    vec = pl.BlockSpec((None, 1, d), lambda i: (layer, 0, 0))
    return pl.pallas_call(
        functools.partial(_outproj_kernel, alpha=alpha), grid=(mt // tm,),
        in_specs=[row(wn), row(wdw), row(wsw),
                  pl.BlockSpec((None, wn, d), lambda i: (layer, 0, 0)),
                  pl.BlockSpec((None, wdw, d), lambda i: (layer, 1, 0)),
                  pl.BlockSpec((None, wsw, d), lambda i: (layer, (wn + wdw) // wsw, 0)),
                  row(d), vec, vec,
                  pl.BlockSpec((None, d, LANE), lambda i: (layer, 0, 0)),
                  pl.BlockSpec((None, 1, LANE), lambda i: (layer, 0, 0))],
        out_specs=[row(d), row(d // 2), row(LANE)],
        out_shape=[jax.ShapeDtypeStruct((mt, d), F32), jax.ShapeDtypeStruct((mt, d // 2), jnp.uint32),
                   jax.ShapeDtypeStruct((mt, LANE), F32)],
        compiler_params=_cparams(("arbitrary",), 48), name="out_proj_ln")(
            hn, hd, hs, w_out_b, w_out_b, w_out_b, x, g, b, rw_pad, rb_pad)


def _route_kernel(lg_ref, e_ref, r_ref, g_ref, cnt_ref, carry, *, tb, n_valid):
    i = pl.program_id(0)

    @pl.when(i == 0)
    def _():
        carry[...] = jnp.zeros_like(carry)

    work = lg_ref[...]
    lane = lax.broadcasted_iota(I32, (tb, LANE), 1)
    lane_f = lane.astype(F32)
    row = i * tb + lax.broadcasted_iota(I32, (tb, 1), 0)
    valid = row < n_valid
    vals, idxs = [], []
    for _ in range(TOP_K):
        m = jnp.max(work, axis=-1, keepdims=True)
        idx = jnp.min(jnp.where(work == m, lane_f, float(LANE)), axis=-1, keepdims=True)
        vals.append(m)
        idxs.append(idx)
        work = jnp.where(lane_f == idx, -jnp.inf, work)
    ex = [jnp.exp(v - vals[0]) for v in vals]
    den = ex[0] + ex[1] + ex[2] + ex[3]
    hot = [(lane_f == idx) & valid for idx in idxs]
    a = jnp.zeros((tb, LANE), F32)
    for hk in hot:
        a = a + jnp.where(hk, 1.0, 0.0)
    tri = (lax.broadcasted_iota(I32, (tb, tb), 0) > lax.broadcasted_iota(I32, (tb, tb), 1))
    before = jnp.dot(jnp.where(tri, 1.0, 0.0).astype(BF16), a.astype(BF16), preferred_element_type=F32) + carry[...]
    e_out = jnp.zeros((tb, LANE), F32)
    r_out = jnp.zeros((tb, LANE), F32)
    g_out = jnp.zeros((tb, LANE), F32)
    for k in range(TOP_K):
        rank = jnp.sum(jnp.where(hot[k], before, 0.0), axis=-1, keepdims=True)
        e_out = jnp.where(lane == k, idxs[k], e_out)
        r_out = jnp.where(lane == k, rank, r_out)
        g_out = jnp.where(lane == k, jnp.where(valid, ex[k] / den, 0.0), g_out)
    e_ref[...] = e_out.astype(I32)
    r_ref[...] = r_out.astype(I32)
    g_ref[...] = g_out
    carry[...] = carry[...] + jnp.sum(a, axis=0, keepdims=True)
    cnt_ref[...] = carry[...]


def _route(logits, n_valid):
    mt = logits.shape[0]
    tb = _pick_tile(mt, (256, 128, 8))
    row = pl.BlockSpec((tb, LANE), lambda i: (i, 0))
    return pl.pallas_call(
        functools.partial(_route_kernel, tb=tb, n_valid=n_valid), grid=(mt // tb,),
        in_specs=[row], out_specs=[row, row, row, pl.BlockSpec((1, LANE), lambda i: (0, 0))],
        out_shape=[jax.ShapeDtypeStruct((mt, LANE), I32), jax.ShapeDtypeStruct((mt, LANE), I32),
                   jax.ShapeDtypeStruct((mt, LANE), F32), jax.ShapeDtypeStruct((1, LANE), F32)],
        scratch_shapes=[pltpu.VMEM((1, LANE), F32)],
        compiler_params=_cparams(("arbitrary",)), name="moe_route")(logits)


def _scatter_kernel(dest_ref, x_ref, xs_in, xs_out, sem, *, ts):
    del xs_in
    i = pl.program_id(0)

    def issue(t, c):
        for k in range(TOP_K):
            d = dest_ref[(i * ts + t) * TOP_K + k]
            pltpu.make_async_copy(x_ref.at[pl.ds(t, 1)], xs_out.at[pl.ds(d, 1)], sem).start()
        return c

    lax.fori_loop(0, ts, issue, 0)

    def drain(t, c):
        for k in range(TOP_K):
            pltpu.make_async_copy(x_ref.at[pl.ds(0, 1)], xs_out.at[pl.ds(0, 1)], sem).wait()
        return c

    lax.fori_loop(0, ts, drain, 0)


def _scatter_rows(dest_flat, x, xs_zero):
    mt, d = x.shape
    ts = _pick_tile(mt, (256, 128, 8))
    return pl.pallas_call(
        functools.partial(_scatter_kernel, ts=ts),
        grid_spec=pltpu.PrefetchScalarGridSpec(
            num_scalar_prefetch=1, grid=(mt // ts,),
            in_specs=[pl.BlockSpec((ts, d), lambda i, dest: (i, 0)), pl.BlockSpec(memory_space=pl.ANY)],
            out_specs=pl.BlockSpec(memory_space=pl.ANY),
            scratch_shapes=[pltpu.SemaphoreType.DMA]),
        out_shape=jax.ShapeDtypeStruct(xs_zero.shape, xs_zero.dtype),
        input_output_aliases={2: 0},
        compiler_params=_cparams(("arbitrary",)), name="moe_scatter")(dest_flat, x, xs_zero)


def _weights_changed(be_ref):
    b = pl.program_id(1)
    return (b == 0) | (be_ref[b] != be_ref[jnp.maximum(b - 1, 0)])


def _moe_up_kernel(be_ref, na_ref, x_ref, w_ref, b_ref, p_ref, o_ref, wb_s):
    active = pl.program_id(1) < na_ref[0]

    @pl.when(active & _weights_changed(be_ref))
    def _():
        wb_s[...] = w_ref[...].astype(BF16)

    @pl.when(active)
    def _():
        h = jnp.dot(_unpack_bf16_pairs(x_ref[...]), wb_s[...], preferred_element_type=F32) + b_ref[...]
        glu = jnp.minimum(h, SWIGLU_LIMIT)
        gated = glu * jax.nn.sigmoid(SWIGLU_ALPHA * glu)
        lin1 = jnp.clip(h, -SWIGLU_LIMIT, SWIGLU_LIMIT) + 1.0
        for k in range(o_ref.shape[1] // LANE):
            pieces = []
            for c0 in (2 * k * LANE, (2 * k + 1) * LANE):
                nxt = pltpu.roll(lin1[:, c0:c0 + LANE], LANE - 1, 1)
                pieces.append((gated[:, c0:c0 + LANE] * nxt).astype(BF16))
            prod = jnp.concatenate(pieces, axis=1)
            o_ref[:, k * LANE:(k + 1) * LANE] = jnp.dot(prod, p_ref[...], preferred_element_type=F32).astype(BF16)

    @pl.when(jnp.logical_not(active))
    def _():
        o_ref[...] = jnp.zeros_like(o_ref)


def _moe_up(block_e, n_act, xs, w_up, b_up4, p_even, layer, n_blocks, tme):
    d = w_up.shape[-2]
    f2 = w_up.shape[-1]
    tn2 = _pick_tile(f2, (1024, 512, 256))
    rows = lambda j, b, be, na: (jnp.minimum(b, na[0] - 1), 0)
    return pl.pallas_call(
        _moe_up_kernel,
        grid_spec=pltpu.PrefetchScalarGridSpec(
            num_scalar_prefetch=2, grid=(f2 // tn2, n_blocks),
            in_specs=[pl.BlockSpec((tme, xs.shape[1]), rows),
                      pl.BlockSpec((None, None, d, tn2), lambda j, b, be, na: (layer, be[b], 0, j)),
                      pl.BlockSpec((None, None, 1, tn2), lambda j, b, be, na: (layer, be[b], 0, j)),
                      pl.BlockSpec((2 * LANE, LANE), lambda j, b, be, na: (0, 0))],
            out_specs=pl.BlockSpec((tme, tn2 // 2), lambda j, b, be, na: (b, j)),
            scratch_shapes=[pltpu.VMEM((d, tn2), BF16)]),
        out_shape=jax.ShapeDtypeStruct((n_blocks * tme, f2 // 2), BF16),
        compiler_params=_cparams(("arbitrary", "arbitrary"), 48), name="moe_up")(
            block_e, n_act, xs, w_up, b_up4, p_even)


def _moe_down_kernel(be_ref, na_ref, a_ref, w_ref, b_ref, o_ref, wb_s):
    active = pl.program_id(1) < na_ref[0]

    @pl.when(active & _weights_changed(be_ref))
    def _():
        wb_s[...] = w_ref[...].astype(BF16)

    @pl.when(active)
    def _():
        o_ref[...] = jnp.dot(a_ref[...], wb_s[...], preferred_element_type=F32) + b_ref[...]

    @pl.when(jnp.logical_not(active))
    def _():
        o_ref[...] = jnp.zeros_like(o_ref)


def _moe_down(block_e, n_act, act, w_down, b_down4, layer, n_blocks, tme):
    f = act.shape[1]
    d = w_down.shape[-1]
    tn = _pick_tile(d, (1024, 512, 256, 128))
    rows = lambda j, b, be, na: (jnp.minimum(b, na[0] - 1), 0)
    return pl.pallas_call(
        _moe_down_kernel,
        grid_spec=pltpu.PrefetchScalarGridSpec(
            num_scalar_prefetch=2, grid=(d // tn, n_blocks),
            in_specs=[pl.BlockSpec((tme, f), rows),
                      pl.BlockSpec((None, None, f, tn), lambda j, b, be, na: (layer, be[b], 0, j)),
                      pl.BlockSpec((None, None, 1, tn), lambda j, b, be, na: (layer, be[b], 0, j))],
            out_specs=pl.BlockSpec((tme, tn), lambda j, b, be, na: (b, j)),
            scratch_shapes=[pltpu.VMEM((f, tn), BF16)]),
        out_shape=jax.ShapeDtypeStruct((n_blocks * tme, d), F32),
        compiler_params=_cparams(("arbitrary", "arbitrary"), 48), name="moe_down")(
            block_e, n_act, act, w_down, b_down4)


def _combine_kernel(dest_ref, y_hbm, x_ref, gate_ref, g_ref, b_ref, xf_ref, xb_ref, buf, sem, *, tc, alpha):
    i = pl.program_id(0)

    def issue(t, c):
        for k in range(TOP_K):
            d = dest_ref[(i * tc + t) * TOP_K + k]
            pltpu.make_async_copy(y_hbm.at[pl.ds(d, 1)], buf.at[k, pl.ds(t, 1)], sem).start()
        return c

    lax.fori_loop(0, tc, issue, 0)

    def drain(t, c):
        for k in range(TOP_K):
            pltpu.make_async_copy(y_hbm.at[pl.ds(0, 1)], buf.at[k, pl.ds(0, 1)], sem).wait()
        return c

    lax.fori_loop(0, tc, drain, 0)
    gate = gate_ref[...]
    y = gate[:, 0:1] * buf[0]
    for k in range(1, TOP_K):
        y = y + gate[:, k:k + 1] * buf[k]
    out = _ln_rows(alpha * x_ref[...] + y, g_ref[...], b_ref[...])
    xf_ref[...] = out
    xb_ref[...] = out.astype(BF16)


def _combine(dest_flat, y_rows, x1, gates, g, b, layer, alpha):
    mt, d = x1.shape
    tc = _pick_tile(mt, (128, 8))
    row = lambda w: pl.BlockSpec((tc, w), lambda i, dest: (i, 0))
    vec = pl.BlockSpec((None, 1, d), lambda i, dest: (layer, 0, 0))
    return pl.pallas_call(
        functools.partial(_combine_kernel, tc=tc, alpha=alpha),
        grid_spec=pltpu.PrefetchScalarGridSpec(
            num_scalar_prefetch=1, grid=(mt // tc,),
            in_specs=[pl.BlockSpec(memory_space=pl.ANY), row(d), row(LANE), vec, vec],
            out_specs=[row(d), row(d)],
            scratch_shapes=[pltpu.VMEM((TOP_K, tc, d), F32), pltpu.SemaphoreType.DMA]),
        out_shape=[jax.ShapeDtypeStruct((mt, d), F32), jax.ShapeDtypeStruct((mt, d), BF16)],
        compiler_params=_cparams(("arbitrary",), 32), name="moe_combine_ln")(dest_flat, y_rows, x1, gates, g, b)


def _moe_layer(x1, x1_packed, logits, n_valid, mw, ln_g, ln_b, layer, alpha):
    mt, d = x1.shape
    tme = 256
    n_blocks = -(-(n_valid * TOP_K) // tme) + N_EXPERTS
    e_i, r_i, gates, counts = _route(logits, n_valid)
    cnt = counts[0, :N_EXPERTS].astype(I32)
    padded = (cnt + tme - 1) // tme * tme
    pend = jnp.cumsum(padded)
    pstart = pend - padded
    n_act = (pend[-1] // tme).astype(I32)
    blk = jnp.arange(n_blocks, dtype=I32)
    block_e = jnp.minimum(jnp.sum((blk[:, None] * tme >= pend[None, :]).astype(I32), axis=1), N_EXPERTS - 1)
    block_e = jnp.where(blk < n_act, block_e, block_e[jnp.maximum(n_act - 1, 0)]).astype(I32)
    e4 = e_i[:, :TOP_K]
    dest = pstart[jnp.clip(e4, 0, N_EXPERTS - 1)] + r_i[:, :TOP_K]
    tok = jnp.arange(mt, dtype=I32)[:, None]
    valid = tok < n_valid
    trash = n_blocks * tme + (tok - n_valid) * TOP_K + jnp.arange(TOP_K, dtype=I32)[None, :]
    dest_scatter = jnp.where(valid, dest, trash).astype(I32).reshape(-1)
    dest_gather = jnp.where(valid, dest, 0).astype(I32).reshape(-1)
    r_tot = n_blocks * tme + (mt - n_valid) * TOP_K
    r_tot = -(-r_tot // 8) * 8
    xs = _scatter_rows(dest_scatter, x1_packed, jnp.zeros((r_tot, x1_packed.shape[1]), x1_packed.dtype))
    n_act1 = n_act.reshape(1)
    act = _moe_up(block_e, n_act1, xs, mw["w_up"], mw["b_up"], mw["p_even"], layer, n_blocks, tme)
    y_rows = _moe_down(block_e, n_act1, act, mw["w_down"], mw["b_down"], layer, n_blocks, tme)
    return _combine(dest_gather, y_rows, x1, gates, ln_g, ln_b, layer, alpha)


def _flash_step(qb, kj, vj, bias_list, shift_list, m, l, acc):
    tq = bias_list[0].shape[0]
    s = lax.dot_general(qb, kj, NT_DIMS, preferred_element_type=F32)
    t = jnp.concatenate([s[r * tq:(r + 1) * tq] + b for r, b in enumerate(bias_list)], axis=0)
    shift = jnp.concatenate([jnp.full((tq, 1), 1.0, F32) * sh for sh in shift_list], axis=0)
    m_new = jnp.maximum(m, jnp.max(t, axis=-1, keepdims=True) + shift)
    a = jnp.exp(m - m_new)
    p = jnp.exp(t - (m_new - shift))
    l = a * l + jnp.sum(p, axis=-1, keepdims=True)
    acc = a * acc + jnp.dot(p.astype(BF16), vj, preferred_element_type=F32)
    return m_new, l, acc


def _nsa_kernel(q_ref, ck_ref, cv_ref, sk_ref, sv_ref, wk_ref, wv_ref, gt_ref, wpk_ref, wpv_ref, e_ref,
                o_ref, kcmp_s, vcmp_s, skb, svb, wkb, wvb, km_s, cnt_s, *, t_len, tq, tk):
    g = pl.program_id(1)
    i = pl.program_id(2)
    nb = t_len // NSA_BLOCK
    n_sel = min(N_SELECT, -(-t_len // NSA_BLOCK))

    @pl.when(i == 0)
    def _():
        ck = ck_ref[...].reshape(nb, NSA_BLOCK, HEAD_DIM)
        kcmp_s[...] = jnp.sum(ck * wpk_ref[...][None, :, :], axis=1).astype(BF16)
        cv = cv_ref[...].reshape(nb, NSA_BLOCK, HEAD_DIM)
        vcmp_s[...] = jnp.sum(cv * wpv_ref[...][None, :, :], axis=1).astype(BF16)
        skb[...] = sk_ref[...].astype(BF16)
        svb[...] = sv_ref[...].astype(BF16)
        wkb[...] = wk_ref[...].astype(BF16)
        wvb[...] = wv_ref[...].astype(BF16)

    s0 = i * tq
    slopes = [jnp.where(g == 0, float(SL_A[0][r]), float(SL_A[1][r])) for r in range(A_REP)]
    q = q_ref[...]
    q3b = (jnp.concatenate([q[:, r * HEAD_DIM:(r + 1) * HEAD_DIM] for r in range(A_REP)], axis=0)
           * (HEAD_DIM ** -0.5)).astype(BF16)
    qpos = s0 + lax.broadcasted_iota(I32, (tq, 1), 0)

    sc = lax.dot_general(q3b, kcmp_s[...], NT_DIMS, preferred_element_type=F32)
    blk = lax.broadcasted_iota(I32, (tq, nb), 1)
    endj = blk * NSA_BLOCK + (NSA_BLOCK - 1)
    okc = endj <= qpos
    distc = (qpos - endj).astype(F32)
    imp = jnp.zeros((tq, nb), F32)
    o_cmp = []
    for r in range(A_REP):
        sm = jnp.where(okc, sc[r * tq:(r + 1) * tq] - slopes[r] * distc, NEG)
        m = jnp.max(sm, axis=-1, keepdims=True)
        e = jnp.where(okc, jnp.exp(sm - m), 0.0)
        p = e / jnp.maximum(jnp.sum(e, axis=-1, keepdims=True), 1e-30)
        imp = imp + p
        o_cmp.append(jnp.dot(p.astype(BF16), vcmp_s[...], preferred_element_type=F32))

    cur = lax.shift_right_logical(qpos, int(math.log2(NSA_BLOCK)))
    score = jnp.where(blk == cur, jnp.inf, jnp.where(blk > cur, -jnp.inf, imp))
    n_live = (s0 + tq) // NSA_BLOCK
    cnt_s[...] = jnp.zeros((tq, nb), F32)
    grp = 8
    for c0 in range(0, nb, grp):
        @pl.when(c0 < n_live)
        def _(c0=c0):
            part = jnp.zeros((tq, nb), F32)
            for c in range(c0, min(c0 + grp, nb)):
                col = score[:, c:c + 1]
                beats = (col > score) | ((col == score) & (blk > c))
                part = part + jnp.where(beats, 1.0, 0.0)
            cnt_s[...] += part
    sel = jnp.where(cnt_s[...] < n_sel, 1.0, 0.0).astype(BF16)
    km = jnp.dot(sel, e_ref[...], preferred_element_type=F32)
    for c in range(t_len // tk):
        km_s[c] = km[:, c * tk:(c + 1) * tk]

    rc = (lax.broadcasted_iota(I32, (tq, tk), 0) - lax.broadcasted_iota(I32, (tq, tk), 1)).astype(F32)
    bias0 = [-slopes[r] * rc for r in range(A_REP)]
    m0 = jnp.full((A_REP * tq, 1), NEG, F32)
    l0 = jnp.zeros((A_REP * tq, 1), F32)
    a0 = jnp.zeros((A_REP * tq, HEAD_DIM), F32)
    j_diag = s0 // tk

    def slc_step(j, carry, causal):
        off = pl.multiple_of(j * tk, tk)
        delta = (s0 - j * tk).astype(F32)
        ok = km_s[j] > 0.5
        if causal:
            ok = ok & (rc + delta >= 0.0)
        bias = [jnp.where(ok, bias0[r], NEG) for r in range(A_REP)]
        shift = [-slopes[r] * delta for r in range(A_REP)]
        return _flash_step(q3b, skb[pl.ds(off, tk), :], svb[pl.ds(off, tk), :], bias, shift, *carry)

    carry = lax.fori_loop(0, j_diag, lambda j, c: slc_step(j, c, False), (m0, l0, a0))
    _, l_s, acc_s = slc_step(j_diag, carry, True)
    o_slc = acc_s / l_s

    def win_body(j, carry):
        off = pl.multiple_of(j * tk, tk)
        delta = (s0 - j * tk).astype(F32)
        dist = rc + delta
        ok = (dist >= 0.0) & (dist < float(WINDOW))
        bias = [jnp.where(ok, bias0[r], NEG) for r in range(A_REP)]
        shift = [-slopes[r] * delta for r in range(A_REP)]
        return _flash_step(q3b, wkb[pl.ds(off, tk), :], wvb[pl.ds(off, tk), :], bias, shift, *carry)

    j_lo = jnp.maximum(s0 - (WINDOW - 1), 0) // tk
    _, l_w, acc_w = lax.fori_loop(j_lo, j_diag + 1, win_body, (m0, l0, a0))
    o_win = acc_w / l_w

    gt = jax.nn.sigmoid(gt_ref[...])
    for r in range(A_REP):
        c = []
        for n in range(N_GATES):
            i0 = (0 * A_REP + r) * N_GATES + n
            i1 = (1 * A_REP + r) * N_GATES + n
            c.append(jnp.where(g == 0, gt[:, i0:i0 + 1], gt[:, i1:i1 + 1]))
        sl = slice(r * tq, (r + 1) * tq)
        out = c[0] * o_cmp[r] + c[1] * o_slc[sl] + c[2] * o_win[sl]
        o_ref[:, r * HEAD_DIM:(r + 1) * HEAD_DIM] = out.astype(BF16)


def _nsa_prompt(proj, cmp_wk, cmp_wv, e_mat, layer, n_b, t_len):
    tq = 128
    tk = _pick_tile(t_len, PROMPT_TK)
    nq = t_len // tq
    nb = t_len // NSA_BLOCK
    gw = A_REP * HEAD_DIM
    seq = lambda cb: pl.BlockSpec((t_len, HEAD_DIM), lambda b, g, i, cb=cb: (b, cb + g))
    wp = pl.BlockSpec((None, NSA_BLOCK, HEAD_DIM), lambda b, g, i: (layer, 0, 0))
    return pl.pallas_call(
        functools.partial(_nsa_kernel, t_len=t_len, tq=tq, tk=tk), grid=(n_b, A_KV, nq),
        in_specs=[pl.BlockSpec((tq, gw), lambda b, g, i: (b * nq + i, g)),
                  seq(CB_NSA), seq(CB_NSA + 2), seq(CB_NSA + 4), seq(CB_NSA + 6), seq(CB_WIN), seq(CB_WIN + 2),
                  pl.BlockSpec((tq, LANE), lambda b, g, i: (b * nq + i, CB_GATE)),
                  wp, wp, pl.BlockSpec((nb, t_len), lambda b, g, i: (0, 0))],
        out_specs=pl.BlockSpec((tq, gw), lambda b, g, i: (b * nq + i, g)),
        out_shape=jax.ShapeDtypeStruct((n_b * t_len, A_HEADS * HEAD_DIM), BF16),
        scratch_shapes=[pltpu.VMEM((nb, HEAD_DIM), BF16), pltpu.VMEM((nb, HEAD_DIM), BF16),
                        pltpu.VMEM((t_len, HEAD_DIM), BF16), pltpu.VMEM((t_len, HEAD_DIM), BF16),
                        pltpu.VMEM((t_len, HEAD_DIM), BF16), pltpu.VMEM((t_len, HEAD_DIM), BF16),
                        pltpu.VMEM((t_len // tk, tq, tk), F32), pltpu.VMEM((tq, nb), F32)],
        compiler_params=_cparams(("arbitrary", "arbitrary", "arbitrary"), 56), name="nsa_prompt")(
            proj, proj, proj, proj, proj, proj, proj, proj, cmp_wk, cmp_wv, e_mat)


DIFF_HEADS_PER_STEP = 2


def _diff_kernel(lam_ref, q_ref, k_ref, v_ref, g_ref, o_ref, kb, vb, *, tq, tk, lam_init):
    hp = pl.program_id(1)
    i = pl.program_id(2)
    nh = DIFF_HEADS_PER_STEP

    @pl.when(i == 0)
    def _():
        kb[...] = k_ref[...].astype(BF16)
        vb[...] = v_ref[...].astype(BF16)

    s0 = i * tq
    lane = lax.broadcasted_iota(I32, (tq, HEAD_DIM), 1)
    rc = (lax.broadcasted_iota(I32, (tq, tk), 0) - lax.broadcasted_iota(I32, (tq, tk), 1)).astype(F32)
    j_diag = s0 // tk
    slopes, qs, bias0 = [], [], []
    for u in range(nh):
        cols = slice(u * HEAD_DIM, (u + 1) * HEAD_DIM)
        slope = _select_by_index(hp, [float(SL_B[nh * j + u]) for j in range(B_HEADS // nh)])
        q = q_ref[:, cols] * (B_HALF ** -0.5)
        qs.append(jnp.concatenate([jnp.where(lane < B_HALF, q, 0.0), jnp.where(lane >= B_HALF, q, 0.0)],
                                  axis=0).astype(BF16))
        slopes.append(slope)
        bias0.append(-slope * rc)

    def step(j, carries, causal):
        off = pl.multiple_of(j * tk, tk)
        delta = (s0 - j * tk).astype(F32)
        out = []
        for u in range(nh):
            cols = slice(u * HEAD_DIM, (u + 1) * HEAD_DIM)
            bias = jnp.where(rc + delta >= 0.0, bias0[u], NEG) if causal else bias0[u]
            shift = -slopes[u] * delta
            out.append(_flash_step(qs[u], kb[pl.ds(off, tk), cols], vb[pl.ds(off, tk), cols], [bias, bias],
                                   [shift, shift], *carries[u]))
        return tuple(out)

    m0 = jnp.full((2 * tq, 1), NEG, F32)
    l0 = jnp.zeros((2 * tq, 1), F32)
    a0 = jnp.zeros((2 * tq, HEAD_DIM), F32)
    carries = lax.fori_loop(0, j_diag, lambda j, c: step(j, c, False), tuple((m0, l0, a0) for _ in range(nh)))
    carries = step(j_diag, carries, True)
    for u in range(nh):
        _, l, acc = carries[u]
        o = acc / l
        o = o[:tq] - lam_ref[0] * o[tq:]
        o = o * lax.rsqrt(jnp.mean(o * o, axis=-1, keepdims=True) + RMS_EPS) * g_ref[...] * (1.0 - lam_init)
        o_ref[:, u * HEAD_DIM:(u + 1) * HEAD_DIM] = o.astype(BF16)


def _diff_prompt(lam, proj, diff_g, layer, n_b, t_len, lam_init):
    tq = 128
    tk = _pick_tile(t_len, PROMPT_TK)
    nq = t_len // tq
    nh = DIFF_HEADS_PER_STEP
    gw = nh * HEAD_DIM
    assert B_HEADS % nh == 0 and CB_BQ % nh == 0 and CB_BK % nh == 0 and CB_BV % nh == 0
    seq = lambda cb: pl.BlockSpec((t_len, gw), lambda b, h, i, cb=cb: (b, cb // nh + h))
    return pl.pallas_call(
        functools.partial(_diff_kernel, tq=tq, tk=tk, lam_init=lam_init), grid=(n_b, B_HEADS // nh, nq),
        in_specs=[pl.BlockSpec(memory_space=pltpu.SMEM),
                  pl.BlockSpec((tq, gw), lambda b, h, i: (b * nq + i, CB_BQ // nh + h)),
                  seq(CB_BK), seq(CB_BV),
                  pl.BlockSpec((None, 1, HEAD_DIM), lambda b, h, i: (layer, 0, 0))],
        out_specs=pl.BlockSpec((tq, gw), lambda b, h, i: (b * nq + i, h)),
        out_shape=jax.ShapeDtypeStruct((n_b * t_len, B_HEADS * HEAD_DIM), BF16),
        scratch_shapes=[pltpu.VMEM((t_len, gw), BF16), pltpu.VMEM((t_len, gw), BF16)],
        compiler_params=_cparams(("arbitrary", "arbitrary", "arbitrary"), 40), name="diff_prompt")(
            lam, proj, proj, proj, diff_g)


def _sgu_kernel(u_ref, v_ref, g_ref, b_ref, w_ref, bt_ref, o_ref, *, n_chunks):
    u = jax.nn.gelu(u_ref[...])
    v = _ln_rows(jax.nn.gelu(v_ref[...]), g_ref[...], b_ref[...])
    tri = lax.broadcasted_iota(I32, (CHUNK, CHUNK), 0) >= lax.broadcasted_iota(I32, (CHUNK, CHUNK), 1)
    for gi in range(C_GROUPS):
        w = jnp.where(tri, w_ref[gi], 0.0).astype(BF16)
        cols = slice(gi * HEAD_DIM, (gi + 1) * HEAD_DIM)
        for c in range(n_chunks):
            rows = slice(c * CHUNK, (c + 1) * CHUNK)
            s = jnp.dot(w, v[rows, cols].astype(BF16), preferred_element_type=F32) + bt_ref[:, gi:gi + 1]
            o_ref[rows, cols] = (u[rows, cols] * s).astype(BF16)


def _sgu_prompt(proj, ln_g, ln_b, sgu_w, sgu_bt, layer, m_p):
    n_chunks = _pick_tile(m_p // CHUNK, (4, 2, 1))
    tr = n_chunks * CHUNK
    vec = pl.BlockSpec((None, 1, C_WIDTH), lambda i: (layer, 0, 0))
    return pl.pallas_call(
        functools.partial(_sgu_kernel, n_chunks=n_chunks), grid=(m_p // tr,),
        in_specs=[pl.BlockSpec((tr, C_WIDTH), lambda i: (i, CB_CU // C_GROUPS)),
                  pl.BlockSpec((tr, C_WIDTH), lambda i: (i, CB_CV // C_GROUPS)),
                  vec, vec,
                  pl.BlockSpec((None, C_GROUPS, CHUNK, CHUNK), lambda i: (layer, 0, 0, 0)),
                  pl.BlockSpec((None, CHUNK, C_GROUPS), lambda i: (layer, 0, 0))],
        out_specs=pl.BlockSpec((tr, C_WIDTH), lambda i: (i, 0)),
        out_shape=jax.ShapeDtypeStruct((m_p, C_WIDTH), BF16),
        compiler_params=_cparams(("arbitrary",)), name="sgu_prompt")(proj, proj, ln_g, ln_b, sgu_w, sgu_bt)


NSA_SLABS = 4 * A_KV
DIFF_SLABS = 2 * B_HEADS
WIN_SLABS = 2 * A_KV


def _page_specs(n_pages_step, rows, layer):
    return [pl.BlockSpec((None, None, rows, HEAD_DIM),
                         lambda b, c, pt, i=i: (layer, pt[b, c * n_pages_step + i], 0, 0))
            for i in range(n_pages_step)]


def _slab(ref, slab, n_slabs):
    return ref[pl.ds(slab, ref.shape[0] // n_slabs, stride=n_slabs), :]


def _pool_kernel(pt_ref, *refs, pg):
    del pt_ref
    pages, wk_ref, wv_ref, o_ref = refs[:pg], refs[pg], refs[pg + 1], refs[pg + 2]
    per_page = pages[0].shape[0] // NSA_SLABS // NSA_BLOCK
    for i in range(pg):
        for slab in range(2 * A_KV):
            x = _slab(pages[i], slab, NSA_SLABS)
            w = wk_ref[...] if slab < A_KV else wv_ref[...]
            for hb in range(per_page):
                r = i * per_page + hb
                o_ref[r:r + 1, slab * HEAD_DIM:(slab + 1) * HEAD_DIM] = jnp.sum(
                    x[hb * NSA_BLOCK:(hb + 1) * NSA_BLOCK] * w, axis=0, keepdims=True)


def _sample_pool(page_table, cache_nsa4, cmp_wk, cmp_wv, layer, pg):
    n_s, n_pages = page_table.shape
    rows = cache_nsa4.shape[2]
    half = 2 * A_KV * HEAD_DIM
    per_page = rows // NSA_SLABS // NSA_BLOCK
    nc = n_pages // pg
    wp = pl.BlockSpec((None, NSA_BLOCK, HEAD_DIM), lambda b, c, pt: (layer, 0, 0))
    return pl.pallas_call(
        functools.partial(_pool_kernel, pg=pg),
        grid_spec=pltpu.PrefetchScalarGridSpec(
            num_scalar_prefetch=1, grid=(n_s, nc),
            in_specs=_page_specs(pg, rows, layer) + [wp, wp],
            out_specs=pl.BlockSpec((None, pg * per_page, half), lambda b, c, pt: (b, c, 0))),
        out_shape=jax.ShapeDtypeStruct((n_s, n_pages * per_page, half), F32),
        compiler_params=_cparams(("arbitrary", "arbitrary"), 32), name="sample_cmp_pool")(
            page_table, *([cache_nsa4] * pg), cmp_wk, cmp_wv)


def _q3_rows(prow, g):
    rowi = lax.broadcasted_iota(I32, (8, HEAD_DIM), 0)
    q3 = jnp.zeros((8, HEAD_DIM), F32)
    for r in range(A_REP):
        c0 = (CB_AQ + g * A_REP + r) * HEAD_DIM
        q3 = jnp.where(rowi == r, jnp.broadcast_to(prow[:, c0:c0 + HEAD_DIM], (8, HEAD_DIM)), q3)
    return q3 * (HEAD_DIM ** -0.5)


def _select_kernel(p_ref, kv_ref, sl_ref, e_ref, ocmp_ref, km_ref, *, p_len, nbp, bpc):
    prow = p_ref[...]
    n_blocks = nbp + 1
    n_sel = min(N_SELECT, n_blocks)
    blk = lax.broadcasted_iota(I32, (8, nbp), 1)
    dist = (p_len - (blk * NSA_BLOCK + NSA_BLOCK - 1)).astype(F32)
    lane_f = lax.broadcasted_iota(I32, (1, nbp), 1).astype(F32)
    rowi = lax.broadcasted_iota(I32, (8, nbp), 0)
    sel_rows = jnp.zeros((8, nbp), F32)
    outs = []
    for g in range(A_KV):
        q3 = _q3_rows(prow, g).astype(BF16)
        kc = kv_ref[:, g * HEAD_DIM:(g + 1) * HEAD_DIM].astype(BF16)
        vc = kv_ref[:, (A_KV + g) * HEAD_DIM:(A_KV + g + 1) * HEAD_DIM].astype(BF16)
        s = lax.dot_general(q3, kc, NT_DIMS, preferred_element_type=F32) - sl_ref[g] * dist
        m = jnp.max(s, axis=-1, keepdims=True)
        e = jnp.exp(s - m)
        p = e / jnp.maximum(jnp.sum(e, axis=-1, keepdims=True), 1e-30)
        o = jnp.dot(p.astype(BF16), vc, preferred_element_type=F32)
        outs += [o[r:r + 1] for r in range(A_REP)]
        work = p[0:1] + p[1:2] + p[2:3]
        sel = jnp.zeros((1, nbp), F32)
        for _ in range(n_sel - 1):
            mx = jnp.max(work, axis=-1, keepdims=True)
            idx = jnp.min(jnp.where(work == mx, lane_f, float(nbp)), axis=-1, keepdims=True)
            hit = lane_f == idx
            sel = jnp.where(hit, 1.0, sel)
            work = jnp.where(hit, -jnp.inf, work)
        sel_rows = jnp.where(rowi == g, sel, sel_rows)
    ocmp_ref[...] = jnp.concatenate(outs, axis=1)
    kpc = bpc * NSA_BLOCK
    for c in range(nbp // bpc):
        in_chunk = (blk >= c * bpc) & (blk < (c + 1) * bpc)
        km_ref[:, c * kpc:(c + 1) * kpc] = jnp.dot(jnp.where(in_chunk, sel_rows, 0.0).astype(BF16), e_ref[...],
                                                   preferred_element_type=F32)


def _sample_select(prow, kvcmp, sl_cols, e_small, p_len, bpc):
    n_s = prow.shape[0]
    nbp = kvcmp.shape[1]
    half = kvcmp.shape[2]
    return pl.pallas_call(
        functools.partial(_select_kernel, p_len=p_len, nbp=nbp, bpc=bpc), grid=(n_s,),
        in_specs=[pl.BlockSpec((None, 1, N_PROJ), lambda b: (b, 0, 0)),
                  pl.BlockSpec((None, nbp, half), lambda b: (b, 0, 0)),
                  pl.BlockSpec((A_KV, 8, 1), lambda b: (0, 0, 0)),
                  pl.BlockSpec((nbp, bpc * NSA_BLOCK), lambda b: (0, 0))],
        out_specs=[pl.BlockSpec((None, 1, A_HEADS * HEAD_DIM), lambda b: (b, 0, 0)),
                   pl.BlockSpec((None, 8, p_len), lambda b: (b, 0, 0))],
        out_shape=[jax.ShapeDtypeStruct((n_s, 1, A_HEADS * HEAD_DIM), F32),
                   jax.ShapeDtypeStruct((n_s, 8, p_len), F32)],
        compiler_params=_cparams(("arbitrary",), 32), name="sample_select")(prow, kvcmp, sl_cols, e_small)


def _decode_update(s, ok, vb_list, m_ref, l_ref, acc_ref, page):
    sm = s if ok is None else jnp.where(ok, s, NEG)
    m_old = m_ref[...]
    m_new = jnp.maximum(m_old, jnp.max(sm, axis=-1, keepdims=True))
    a = jnp.exp(m_old - m_new)
    p = jnp.exp(sm - m_new)
    if ok is not None:
        p = jnp.where(ok, p, 0.0)
    l_ref[...] = a * l_ref[...] + jnp.sum(p, axis=-1, keepdims=True)
    pb = p.astype(BF16)
    pv = jnp.dot(pb[:, 0:page], vb_list[0], preferred_element_type=F32)
    for i in range(1, len(vb_list)):
        pv = pv + jnp.dot(pb[:, i * page:(i + 1) * page], vb_list[i], preferred_element_type=F32)
    acc_ref[...] = a * acc_ref[...] + pv
    m_ref[...] = m_new


def _nsa_decode_kernel(pt_ref, *refs, pg, p_len, n_win):
    del pt_ref
    pages = refs[:pg]
    (p_ref, km_ref, win_ref, ocmp_ref, sl_ref, sg_ref, sb_ref, w00_ref, b0_ref,
     o_ref, osgu_ref, vrow_ref, m_s, l_s, acc_s) = refs[pg:]
    c = pl.program_id(1)
    page = pages[0].shape[0] // NSA_SLABS
    gw = A_KV * HEAD_DIM

    @pl.when(c == 0)
    def _():
        m_s[...] = jnp.full_like(m_s, NEG)
        l_s[...] = jnp.zeros_like(l_s)
        acc_s[...] = jnp.zeros_like(acc_s)

    prow = p_ref[...]
    kpos = c * (pg * page) + lax.broadcasted_iota(I32, (8, pg * page), 1)
    dist = (p_len - kpos).astype(F32)
    for g in range(A_KV):
        q3 = _q3_rows(prow, g).astype(BF16)
        s = jnp.concatenate(
            [lax.dot_general(q3, _slab(pages[i], 2 * A_KV + g, NSA_SLABS).astype(BF16), NT_DIMS,
                             preferred_element_type=F32) for i in range(pg)], axis=1)
        s = s - sl_ref[g] * dist
        ok = jnp.broadcast_to(km_ref[g:g + 1, :], s.shape) > 0.5
        vbs = [_slab(pages[i], 3 * A_KV + g, NSA_SLABS).astype(BF16) for i in range(pg)]
        _decode_update(s, ok, vbs, m_s.at[g], l_s.at[g], acc_s.at[g], page)

    @pl.when(c == pl.num_programs(1) - 1)
    def _():
        gt = jax.nn.sigmoid(prow[:, CB_GATE * LANE:(CB_GATE + 1) * LANE])
        ocmp = ocmp_ref[...]
        wpos = p_len - n_win + lax.broadcasted_iota(I32, (8, n_win), 1)
        wd = (p_len - wpos).astype(F32)
        wok = (wd >= 0.0) & (wd < float(WINDOW))
        outs = []
        for g in range(A_KV):
            q3 = _q3_rows(prow, g)
            q3b = q3.astype(BF16)
            nsa0 = (CB_NSA + 2 * A_KV) * HEAD_DIM
            k_new = prow[:, nsa0 + g * HEAD_DIM:nsa0 + (g + 1) * HEAD_DIM]
            v_new = prow[:, nsa0 + gw + g * HEAD_DIM:nsa0 + gw + (g + 1) * HEAD_DIM]
            s_new = jnp.sum(q3b.astype(F32) * k_new.astype(BF16).astype(F32), axis=-1, keepdims=True)
            m_old = m_s[g]
            m_f = jnp.maximum(m_old, s_new)
            a = jnp.exp(m_old - m_f)
            pn = jnp.exp(s_new - m_f)
            o_slc = (a * acc_s[g] + pn * v_new.astype(BF16).astype(F32)) / (a * l_s[g] + pn)
            wk = _slab(win_ref, g, WIN_SLABS).astype(BF16)
            wv = _slab(win_ref, A_KV + g, WIN_SLABS).astype(BF16)
            win0 = CB_WIN * HEAD_DIM
            wk_new = prow[:, win0 + g * HEAD_DIM:win0 + (g + 1) * HEAD_DIM]
            wv_new = prow[:, win0 + gw + g * HEAD_DIM:win0 + gw + (g + 1) * HEAD_DIM]
            sw = lax.dot_general(q3b, wk, NT_DIMS, preferred_element_type=F32) - sl_ref[g] * wd
            sw = jnp.where(wok, sw, NEG)
            sw_new = jnp.sum(q3b.astype(F32) * wk_new.astype(BF16).astype(F32), axis=-1, keepdims=True)
            mw = jnp.maximum(jnp.max(sw, axis=-1, keepdims=True), sw_new)
            pw = jnp.where(wok, jnp.exp(sw - mw), 0.0)
            pwn = jnp.exp(sw_new - mw)
            o_win = ((jnp.dot(pw.astype(BF16), wv, preferred_element_type=F32) + pwn * wv_new.astype(BF16).astype(F32))
                     / (jnp.sum(pw, axis=-1, keepdims=True) + pwn))
            for r in range(A_REP):
                hd = g * A_REP + r
                gi = hd * N_GATES
                oc = ocmp[:, hd * HEAD_DIM:(hd + 1) * HEAD_DIM]
                outs.append(gt[:, gi:gi + 1] * oc + gt[:, gi + 1:gi + 2] * o_slc[r:r + 1]
                            + gt[:, gi + 2:gi + 3] * o_win[r:r + 1])
        o_ref[...] = jnp.concatenate(outs, axis=1)
        u = jax.nn.gelu(prow[:, CB_CU * LANE:CB_CU * LANE + C_WIDTH])
        v = _ln_rows(jax.nn.gelu(prow[:, CB_CV * LANE:CB_CV * LANE + C_WIDTH]), sg_ref[...], sb_ref[...])
        vrow_ref[...] = v
        osgu_ref[...] = u * (w00_ref[...] * v + b0_ref[...])


def _sample_nsa(page_table, cache_nsa4, prow, km, win_l, ocmp, sl_cols, sgu_g, sgu_b, w00, b0, layer, pg, p_len):
    n_s, n_pages = page_table.shape
    rows = cache_nsa4.shape[2]
    page = rows // NSA_SLABS
    n_win = win_l.shape[2] // WIN_SLABS
    nc = n_pages // pg
    one = lambda w: pl.BlockSpec((None, 1, w), lambda b, c, pt: (b, 0, 0))
    lvec = pl.BlockSpec((None, 1, C_WIDTH), lambda b, c, pt: (layer, 0, 0))
    return pl.pallas_call(
        functools.partial(_nsa_decode_kernel, pg=pg, p_len=p_len, n_win=n_win),
        grid_spec=pltpu.PrefetchScalarGridSpec(
            num_scalar_prefetch=1, grid=(n_s, nc),
            in_specs=_page_specs(pg, rows, layer) + [
                one(N_PROJ),
                pl.BlockSpec((None, 8, pg * page), lambda b, c, pt: (b, 0, c)),
                pl.BlockSpec((None, None, n_win * WIN_SLABS, HEAD_DIM), lambda b, c, pt: (layer, b, 0, 0)),
                one(A_HEADS * HEAD_DIM),
                pl.BlockSpec((A_KV, 8, 1), lambda b, c, pt: (0, 0, 0)),
                lvec, lvec, lvec, lvec],
            out_specs=[one(A_HEADS * HEAD_DIM), one(C_WIDTH), one(C_WIDTH)],
            scratch_shapes=[pltpu.VMEM((A_KV, 8, 1), F32), pltpu.VMEM((A_KV, 8, 1), F32),
                            pltpu.VMEM((A_KV, 8, HEAD_DIM), F32)]),
        out_shape=[jax.ShapeDtypeStruct((n_s, 1, A_HEADS * HEAD_DIM), F32),
                   jax.ShapeDtypeStruct((n_s, 1, C_WIDTH), F32), jax.ShapeDtypeStruct((n_s, 1, C_WIDTH), F32)],
        compiler_params=_cparams(("arbitrary", "arbitrary"), 40), name="sample_nsa")(
            page_table, *([cache_nsa4] * pg), prow, km, win_l, ocmp, sl_cols, sgu_g, sgu_b, w00, b0)


def _diff_decode_kernel(pt_ref, *refs, pg, p_len, lam_init):
    del pt_ref
    pages = refs[:pg]
    p_ref, g_ref, lam_ref, o_ref, m_s, l_s, acc_s = refs[pg:]
    c = pl.program_id(1)
    page = pages[0].shape[0] // DIFF_SLABS

    @pl.when(c == 0)
    def _():
        m_s[...] = jnp.full_like(m_s, NEG)
        l_s[...] = jnp.zeros_like(l_s)
        acc_s[...] = jnp.zeros_like(acc_s)

    prow = p_ref[...]
    hw = B_HEADS * HEAD_DIM
    nr = 16
    q = prow[:, CB_BQ * LANE:CB_BQ * LANE + hw] * (B_HALF ** -0.5)
    seg = lax.shift_right_logical(lax.broadcasted_iota(I32, (nr, hw), 1), int(math.log2(B_HALF)))
    rowi = lax.broadcasted_iota(I32, (nr, hw), 0)
    qbd_b = jnp.where(seg == rowi, jnp.broadcast_to(q, (nr, hw)), 0.0).astype(BF16)
    row1 = lax.shift_right_logical(lax.broadcasted_iota(I32, (nr, 1), 0), 1)
    sl_col = jnp.zeros((nr, 1), F32)
    for h in range(B_HEADS):
        sl_col = jnp.where(row1 == h, float(SL_B[h]), sl_col)
    kpos = c * (pg * page) + lax.broadcasted_iota(I32, (nr, pg * page), 1)
    dist = (p_len - kpos).astype(F32)

    def wide(i, first):
        return jnp.concatenate([_slab(pages[i], 2 * h + first, DIFF_SLABS).astype(BF16) for h in range(B_HEADS)], axis=1)

    s = jnp.concatenate([lax.dot_general(qbd_b, wide(i, 0), NT_DIMS, preferred_element_type=F32)
                         for i in range(pg)], axis=1)
    s = s - sl_col * dist
    _decode_update(s, None, [wide(i, 1) for i in range(pg)], m_s, l_s, acc_s, page)

    @pl.when(c == pl.num_programs(1) - 1)
    def _():
        k_new = prow[:, CB_BK * LANE:CB_BK * LANE + hw].astype(BF16).astype(F32)
        v_new = prow[:, CB_BV * LANE:CB_BV * LANE + hw].astype(BF16).astype(F32)
        s_new = jnp.sum(qbd_b.astype(F32) * k_new, axis=-1, keepdims=True)
        m_old = m_s[...]
        m_f = jnp.maximum(m_old, s_new)
        a = jnp.exp(m_old - m_f)
        pn = jnp.exp(s_new - m_f)
        o = (a * acc_s[...] + pn * v_new) / (a * l_s[...] + pn)
        outs = []
        for h in range(B_HEADS):
            cols = slice(h * HEAD_DIM, (h + 1) * HEAD_DIM)
            oh = o[2 * h:2 * h + 1, cols] - lam_ref[0] * o[2 * h + 1:2 * h + 2, cols]
            oh = oh * lax.rsqrt(jnp.mean(oh * oh, axis=-1, keepdims=True) + RMS_EPS) * g_ref[...] * (1.0 - lam_init)
            outs.append(oh)
        o_ref[...] = jnp.concatenate(outs, axis=1)


def _sample_diff(page_table, lam, cache_diff4, prow, diff_g, layer, pg, p_len, lam_init):
    n_s, n_pages = page_table.shape
    rows = cache_diff4.shape[2]
    hw = B_HEADS * HEAD_DIM
    nc = n_pages // pg
    return pl.pallas_call(
        functools.partial(_diff_decode_kernel, pg=pg, p_len=p_len, lam_init=lam_init),
        grid_spec=pltpu.PrefetchScalarGridSpec(
            num_scalar_prefetch=1, grid=(n_s, nc),
            in_specs=_page_specs(pg, rows, layer) + [
                pl.BlockSpec((None, 1, N_PROJ), lambda b, c, pt: (b, 0, 0)),
                pl.BlockSpec((None, 1, HEAD_DIM), lambda b, c, pt: (layer, 0, 0)),
                pl.BlockSpec(memory_space=pltpu.SMEM)],
            out_specs=pl.BlockSpec((None, 1, hw), lambda b, c, pt: (b, 0, 0)),
            scratch_shapes=[pltpu.VMEM((16, 1), F32), pltpu.VMEM((16, 1), F32), pltpu.VMEM((16, hw), F32)]),
        out_shape=jax.ShapeDtypeStruct((n_s, 1, hw), F32),
        compiler_params=_cparams(("arbitrary", "arbitrary"), 40), name="sample_diff")(
            page_table, *([cache_diff4] * pg), prow, diff_g, lam)


def kernel(x_prompt, x_sample, cache_nsa, cache_diff, state_win, page_table, ln_in_g, ln_in_b, w_in, cmp_wk, cmp_wv,
           lam_q1, lam_k1, lam_q2, lam_k2, diff_norm_g, sgu_ln_g, sgu_ln_b, sgu_w, sgu_b, w_out, ln1_g, ln1_b,
           router_w, router_b, w_up, b_up, w_down, b_down, ln2_g, ln2_b):
    n_b, t_len, d = x_prompt.shape
    n_s, t_s, _ = x_sample.shape
    depth = w_in.shape[0]
    assert t_s == 1 and t_len % CHUNK == 0
    n_pages = page_table.shape[1]
    page = cache_nsa.shape[2]
    p_len = n_pages * page
    n_win = state_win.shape[2]
    assert p_len % CHUNK == 0 and page % NSA_BLOCK == 0 and n_win == min(WINDOW, p_len)
    m_p = n_b * t_len
    n_valid = m_p + n_s
    mt = -(-n_valid // 256) * 256
    alpha = (2 * depth) ** 0.25
    pg = _pick_tile(n_pages, (8, 4, 2, 1))
    bpc = pg * (page // NSA_BLOCK)

    w_in_r = jnp.concatenate(
        [w_in[:, :, :GATE_ORIG], w_in[:, :, GATE_ORIG + N_GATE_COLS:], w_in[:, :, GATE_ORIG:GATE_ORIG + N_GATE_COLS],
         jnp.zeros((depth, d, LANE - N_GATE_COLS), w_in.dtype)], axis=-1).astype(BF16)
    assert w_in_r.shape[-1] == N_PROJ
    w_out_b = w_out.astype(BF16)
    p_even = jnp.asarray(np.arange(2 * LANE)[:, None] == 2 * np.arange(LANE)[None, :], BF16)
    mw = {"w_up": w_up, "b_up": b_up[:, :, None, :], "w_down": w_down, "b_down": b_down[:, :, None, :],
          "p_even": p_even}
    rw_pad = jnp.pad(router_w, ((0, 0), (0, 0), (0, LANE - N_EXPERTS)))
    rb_pad = jnp.pad(router_b, ((0, 0), (0, LANE - N_EXPERTS)), constant_values=NEG)[:, None, :]
    ln1_g3, ln1_b3, ln2_g3, ln2_b3 = (a[:, None, :] for a in (ln1_g, ln1_b, ln2_g, ln2_b))
    sgu_g3, sgu_b3 = sgu_ln_g[:, None, :], sgu_ln_b[:, None, :]
    sgu_bt = jnp.swapaxes(sgu_b, 1, 2)
    sgu_w00 = jnp.repeat(sgu_w[:, :, 0, 0], HEAD_DIM, axis=1)[:, None, :]
    sgu_b0 = jnp.repeat(sgu_b[:, :, 0], HEAD_DIM, axis=1)[:, None, :]
    diff_g3 = diff_norm_g[:, None, :]
    nb_p = t_len // NSA_BLOCK
    e_prompt = jnp.asarray((np.arange(t_len)[None, :] // NSA_BLOCK == np.arange(nb_p)[:, None]), BF16)
    e_small = jnp.asarray((np.arange(bpc * NSA_BLOCK)[None, :] // NSA_BLOCK == np.arange(p_len // NSA_BLOCK)[:, None] % bpc), BF16)
    sl_a_cols = jnp.asarray(np.concatenate([SL_A, np.zeros((A_KV, 8 - A_REP), np.float32)], axis=1)[:, :, None])
    cache_nsa4 = cache_nsa.reshape(depth, cache_nsa.shape[1], page * NSA_SLABS, HEAD_DIM)
    cache_diff4 = jnp.swapaxes(cache_diff, 3, 4).reshape(depth, cache_diff.shape[1], page * DIFF_SLABS, HEAD_DIM)
    win4 = state_win.reshape(depth, n_s, n_win * WIN_SLABS, HEAD_DIM)

    x_all = jnp.concatenate([x_prompt.reshape(m_p, d), x_sample.reshape(n_s, d),
                             jnp.zeros((mt - n_valid, d), x_prompt.dtype)], axis=0)
    xf, xb = _ln_in(x_all, ln_in_g, ln_in_b)

    nsa_p, nsa_s, diff_p, diff_s, win_p, win_s, sgu_s = [], [], [], [], [], [], []
    for l in range(depth):
        lam_init = 0.8 - 0.6 * math.exp(-0.3 * l)
        lam = (jnp.exp(jnp.sum(lam_q1[l] * lam_k1[l])) - jnp.exp(jnp.sum(lam_q2[l] * lam_k2[l])) + lam_init)
        lam = lam.astype(F32).reshape(1)
        proj = _proj(xb, w_in_r, l)
        prow = proj[m_p:n_valid].reshape(n_s, 1, N_PROJ)

        o_nsa = _nsa_prompt(proj, cmp_wk, cmp_wv, e_prompt, l, n_b, t_len)
        o_diff = _diff_prompt(lam, proj, diff_g3, l, n_b, t_len, lam_init)
        o_sgu = _sgu_prompt(proj, sgu_g3, sgu_b3, sgu_w, sgu_bt, l, m_p)

        kvcmp = _sample_pool(page_table, cache_nsa4, cmp_wk, cmp_wv, l, pg)
        ocmp_s, km = _sample_select(prow, kvcmp, sl_a_cols, e_small, p_len, bpc)
        o_nsa_s, o_sgu_s, v_rows = _sample_nsa(page_table, cache_nsa4, prow, km, win4, ocmp_s, sl_a_cols,
                                               sgu_g3, sgu_b3, sgu_w00, sgu_b0, l, pg, p_len)
        o_diff_s = _sample_diff(page_table, lam, cache_diff4, prow, diff_g3, l, pg, p_len, lam_init)

        pad = lambda w: jnp.zeros((mt - n_valid, w), BF16)
        hn = jnp.concatenate([o_nsa, o_nsa_s.reshape(n_s, -1).astype(BF16), pad(o_nsa.shape[1])], axis=0)
        hd = jnp.concatenate([o_diff, o_diff_s.reshape(n_s, -1).astype(BF16), pad(o_diff.shape[1])], axis=0)
        hs = jnp.concatenate([o_sgu, o_sgu_s.reshape(n_s, -1).astype(BF16), pad(o_sgu.shape[1])], axis=0)
        x1, x1_packed, logits = _outproj(hn, hd, hs, w_out_b, xf, ln1_g3, ln1_b3, rw_pad, rb_pad, l, alpha)
        xf, xb = _moe_layer(x1, x1_packed, logits, n_valid, mw, ln2_g3, ln2_b3, l, alpha)

        c0, c1 = CB_NSA * LANE, CB_WIN * LANE
        nsa_p.append(proj[:m_p, c0:c1].reshape(n_b, t_len, 4, A_KV, HEAD_DIM))
        nsa_s.append(proj[m_p:n_valid, c0:c1].reshape(n_s, 1, 4, A_KV, HEAD_DIM))
        d0, d1 = CB_BK * LANE, CB_CU * LANE
        diff_p.append(proj[:m_p, d0:d1].reshape(n_b, t_len, 2, B_HEADS, HEAD_DIM))
        diff_s.append(proj[m_p:n_valid, d0:d1].reshape(n_s, 1, 2, B_HEADS, HEAD_DIM))
        w_rows = proj[:m_p, c1:CB_BQ * LANE].reshape(n_b, t_len, 2, A_KV, HEAD_DIM)
        win_p.append(w_rows[:, t_len - min(WINDOW, t_len):])
        w_new = proj[m_p:n_valid, c1:CB_BQ * LANE].reshape(n_s, 1, 2, A_KV, HEAD_DIM)
        win_s.append(jnp.concatenate([state_win[l], w_new], axis=1)[:, 1:])
        sgu_s.append(v_rows.reshape(n_s, 1, C_WIDTH))

    y_p = xf[:m_p].reshape(n_b, t_len, d)
    y_s = xf[m_p:n_valid].reshape(n_s, 1, d)
    return (y_p, y_s, jnp.stack(nsa_p), jnp.stack(nsa_s), jnp.stack(diff_p), jnp.stack(diff_s),
            jnp.stack(win_p), jnp.stack(win_s), jnp.stack(sgu_s))
```
